```python
import math
import jax, jax.numpy as jnp
from jax import lax
import numpy as np

D_MODEL = 4096
BATCH = 1
SEQ = 16384
DEPTH = 4

CHUNK = 64
N_META = 16
FRONT_PAD = (-N_META) % CHUNK
NORM_EPS = 1e-6
F32 = jnp.float32

SSD_HEADDIM = 64
SSD_WIDTH = D_MODEL
SSD_HEADS = SSD_WIDTH // SSD_HEADDIM
SSD_GROUPS = 8
SSD_STATE = 128
SSD_CONV = 4
SSD_CONV_DIM = SSD_WIDTH + 2 * SSD_GROUPS * SSD_STATE

S5_GROUP = 16
S5_WIDTH = D_MODEL // 2
S5_GROUPS = S5_WIDTH // S5_GROUP
S5_STATE = 64

HYB_IN = SSD_WIDTH + SSD_CONV_DIM + SSD_HEADS + 2 * S5_WIDTH
HYB_SPLITS = (SSD_WIDTH, SSD_WIDTH + SSD_CONV_DIM, SSD_WIDTH + SSD_CONV_DIM + SSD_HEADS,
              SSD_WIDTH + SSD_CONV_DIM + SSD_HEADS + S5_WIDTH)
HYB_OUT = SSD_WIDTH + S5_WIDTH

MLA_HEADS = D_MODEL // 128
MLA_Q_RANK = D_MODEL // 4
MLA_KV_RANK = D_MODEL // 8
MLA_NOPE = 128
MLA_ROPE = 64
MLA_V = 128
MLA_WIDTH = MLA_HEADS * MLA_V
MLA_IN = MLA_Q_RANK + MLA_KV_RANK + MLA_ROPE + MLA_WIDTH
MLA_SPLITS = (MLA_Q_RANK, MLA_Q_RANK + MLA_KV_RANK, MLA_Q_RANK + MLA_KV_RANK + MLA_ROPE)
ROPE_BASE = 10000.0
Q_BLOCK = 128

kernel_name = 'hybrid_ssd_s5_mla_meta_stream'


def rms_norm(x, g):
    xf = x.astype(F32)
    y = xf * lax.rsqrt(jnp.mean(xf * xf, axis=-1, keepdims=True) + NORM_EPS)
    return (y * g.astype(F32)).astype(x.dtype)


def chunk_ids(length):
    p = jnp.arange(length)
    return jnp.where(p < N_META, 0, 1 + (p - N_META) // CHUNK)


def causal_conv(x, w, b):
    y = lax.conv_general_dilated(x, w[:, None, :].astype(x.dtype), window_strides=(1,),
                                 padding=[(SSD_CONV - 1, 0)],
                                 dimension_numbers=('NWC', 'WIO', 'NWC'),
                                 feature_group_count=x.shape[-1])
    return y + b.astype(x.dtype)


def ssd_scan(xs, dt, a, bm, cm):
    bsz = xs.shape[0]
    f = FRONT_PAD
    padf = lambda t: jnp.pad(t, [(0, 0), (f, 0)] + [(0, 0)] * (t.ndim - 2))
    xs, dt, bm, cm = padf(xs), padf(dt), padf(bm), padf(cm)
    nc = xs.shape[1] // CHUNK
    e = SSD_HEADS // SSD_GROUPS
    x = (xs * dt[..., None]).reshape(bsz, nc, CHUNK, SSD_GROUPS, e, SSD_HEADDIM)
    da = (dt * a).reshape(bsz, nc, CHUNK, SSD_GROUPS, e)
    bm = bm.reshape(bsz, nc, CHUNK, SSD_GROUPS, SSD_STATE)
    cm = cm.reshape(bsz, nc, CHUNK, SSD_GROUPS, SSD_STATE)
    acum = jnp.cumsum(da, axis=2)
    seg = acum[:, :, :, None] - acum[:, :, None, :]
    causal = jnp.tril(jnp.ones((CHUNK, CHUNK), bool))[:, :, None, None]
    decay = jnp.exp(jnp.where(causal, seg, -jnp.inf))
    cb = jnp.einsum('bclgn,bcsgn->bclsg', cm, bm)
    y_diag = jnp.einsum('bclsg,bclsge,bcsgep->bclgep', cb, decay, x)
    decay_to_end = jnp.exp(acum[:, :, -1:] - acum)
    states = jnp.einsum('bclgn,bclge,bclgep->bcgepn', bm, decay_to_end, x)
    chunk_decay = jnp.exp(acum[:, :, -1])

    def step(h, inp):
        s, d = inp
        return h * d[..., None, None] + s, h

    _, prev = lax.scan(step, jnp.zeros_like(states[:, 0]),
                       (jnp.moveaxis(states, 1, 0), jnp.moveaxis(chunk_decay, 1, 0)))
    prev = jnp.moveaxis(prev, 0, 1)
    y_off = jnp.einsum('bclgn,bcgepn,bclge->bclgep', cm, prev, jnp.exp(acum))
    y = (y_diag + y_off).reshape(bsz, nc * CHUNK, SSD_HEADS, SSD_HEADDIM)
    return y[:, f:]


def ssd_branch(z, xbc, dt_raw, conv_w, conv_b, dt_bias, a_log, d_skip, norm_g):
    bsz, length, _ = z.shape
    xbc = jax.nn.silu(causal_conv(xbc, conv_w, conv_b)).astype(F32)
    xs, bm, cm = jnp.split(xbc, [SSD_WIDTH, SSD_WIDTH + SSD_GROUPS * SSD_STATE], axis=-1)
    xs = xs.reshape(bsz, length, SSD_HEADS, SSD_HEADDIM)
    bm = bm.reshape(bsz, length, SSD_GROUPS, SSD_STATE)
    cm = cm.reshape(bsz, length, SSD_GROUPS, SSD_STATE)
    dt = jax.nn.softplus(dt_raw.astype(F32) + dt_bias.astype(F32))
    a = -jnp.exp(a_log.astype(F32))
    y = ssd_scan(xs, dt, a, bm, cm) + d_skip.astype(F32)[:, None] * xs
    y = y.reshape(bsz, length, SSD_WIDTH) * jax.nn.silu(z.astype(F32))
    yg = y.reshape(bsz, length, SSD_GROUPS, -1)
    yg = yg * lax.rsqrt(jnp.mean(yg * yg, axis=-1, keepdims=True) + NORM_EPS)
    return (yg.reshape(bsz, length, SSD_WIDTH) * norm_g.astype(F32)).astype(z.dtype)


def s5_branch(u, gate, a_re, a_im, log_dt, b_re, b_im, c_re, c_im, d_skip, w_glu):
    bsz, length, _ = u.shape
    a_re, a_im = a_re.astype(F32), a_im.astype(F32)
    b_re, b_im = b_re.astype(F32), b_im.astype(F32)
    c_re, c_im = c_re.astype(F32), c_im.astype(F32)
    uf = u.astype(F32)
    ug = uf.reshape(bsz, length, S5_GROUPS, S5_GROUP)
    dt = jnp.exp(log_dt.astype(F32))[:, None]
    mag = jnp.exp(dt * a_re)
    ab_re, ab_im = mag * jnp.cos(dt * a_im), mag * jnp.sin(dt * a_im)
    den = a_re * a_re + a_im * a_im
    k_re = ((ab_re - 1.0) * a_re + ab_im * a_im) / den
    k_im = (ab_im * a_re - (ab_re - 1.0) * a_im) / den
    bb_re = k_re[..., None] * b_re - k_im[..., None] * b_im
    bb_im = k_re[..., None] * b_im + k_im[..., None] * b_re
    bu_re = jnp.einsum('gpi,blgi->blgp', bb_re, ug)
    bu_im = jnp.einsum('gpi,blgi->blgp', bb_im, ug)
    shp = bu_re.shape

    def combine(e1, e2):
        a1r, a1i, b1r, b1i = e1
        a2r, a2i, b2r, b2i = e2
        return (a2r * a1r - a2i * a1i, a2r * a1i + a2i * a1r,
                a2r * b1r - a2i * b1i + b2r, a2r * b1i + a2i * b1r + b2i)

    _, _, s_re, s_im = lax.associative_scan(
        combine, (jnp.broadcast_to(ab_re, shp), jnp.broadcast_to(ab_im, shp), bu_re, bu_im), axis=1)
    y = jnp.einsum('gip,blgp->blgi', c_re, s_re) - jnp.einsum('gip,blgp->blgi', c_im, s_im)
    y = y.reshape(bsz, length, S5_WIDTH) + d_skip.astype(F32) * uf
    g = jax.nn.gelu(y)
    y = g * jax.nn.sigmoid(g @ w_glu.astype(F32))
    return (y * jax.nn.silu(gate.astype(F32))).astype(u.dtype)


def rope(x, cos, sin):
    x1, x2 = jnp.split(x, 2, axis=-1)
    return jnp.concatenate([x1 * cos - x2 * sin, x2 * cos + x1 * sin], axis=-1)


def chunk_causal_attention(q_nope, q_rope, k_nope, k_rope, v):
    bsz, length = q_nope.shape[:2]
    nblk = -(-length // Q_BLOCK)
    pad = nblk * Q_BLOCK - length
    cid = chunk_ids(length)
    cid_q = jnp.pad(cid, (0, pad), constant_values=0).reshape(nblk, Q_BLOCK)

    def to_blocks(t):
        t = jnp.pad(t, [(0, 0), (0, pad)] + [(0, 0)] * (t.ndim - 2))
        return jnp.moveaxis(t.reshape(bsz, nblk, Q_BLOCK, *t.shape[2:]), 1, 0)

    scale = (MLA_NOPE + MLA_ROPE) ** -0.5

    def one_block(args):
        qn, qr, cq = args
        s = jnp.einsum('bqhd,bkhd->bhqk', qn, k_nope) + jnp.einsum('bqhr,bkr->bhqk', qr, k_rope)
        s = jnp.where(cid[None, :] <= cq[:, None], s.astype(F32) * scale, -jnp.inf)
        p = jax.nn.softmax(s, axis=-1).astype(v.dtype)
        return jnp.einsum('bhqk,bkhd->bqhd', p, v)

    o = lax.map(one_block, (to_blocks(q_nope), to_blocks(q_rope), cid_q))
    o = jnp.moveaxis(o, 0, 1).reshape(bsz, nblk * Q_BLOCK, MLA_HEADS, MLA_V)
    return o[:, :length]


def mla_branch(h, w_in, q_norm, w_uq, kv_norm, w_ukv):
    bsz, length, _ = h.shape
    c_q, c_kv, k_rope, gate = jnp.split(h @ w_in, MLA_SPLITS, axis=-1)
    q = (rms_norm(c_q, q_norm) @ w_uq).reshape(bsz, length, MLA_HEADS, MLA_NOPE + MLA_ROPE)
    kv = (rms_norm(c_kv, kv_norm) @ w_ukv).reshape(bsz, length, MLA_HEADS, MLA_NOPE + MLA_V)
    q_nope, q_rope = q[..., :MLA_NOPE], q[..., MLA_NOPE:]
    k_nope, v = kv[..., :MLA_NOPE], kv[..., MLA_NOPE:]
    pos = jnp.arange(length, dtype=F32)
    inv_freq = ROPE_BASE ** (-jnp.arange(0, MLA_ROPE, 2, dtype=F32) / MLA_ROPE)
    ang = pos[:, None] * inv_freq[None, :]
    cos, sin = jnp.cos(ang).astype(h.dtype), jnp.sin(ang).astype(h.dtype)
    q_rope = rope(q_rope, cos[:, None, :], sin[:, None, :])
    k_rope = rope(k_rope, cos, sin)
    o = chunk_causal_attention(q_nope, q_rope, k_nope, k_rope, v)
    return o.reshape(bsz, length, MLA_WIDTH) * jax.nn.silu(gate)


def setup_inputs(seed: int = 0) -> dict:
    key = jax.random.key(seed)
    ks = iter(jax.random.split(key, 40))
    ne, no = (DEPTH + 1) // 2, DEPTH // 2

    def nrm(shape, scale):
        return jax.random.normal(next(ks), shape, F32) * scale

    def gain(shape):
        return 1.0 + nrm(shape, 0.02)

    def log_uniform(shape, lo, hi):
        return jax.random.uniform(next(ks), shape, F32, math.log(lo), math.log(hi))

    ssd_dt = jnp.exp(log_uniform((ne, SSD_HEADS), 1e-3, 1e-1))
    n_idx = jnp.arange(S5_STATE, dtype=F32)
    return {
        'x': nrm((BATCH, SEQ, D_MODEL), 1.0),
        'meta': nrm((N_META, D_MODEL), 1.0),
        'hyb_norm': gain((ne, D_MODEL)),
        'hyb_w_in': nrm((ne, D_MODEL, HYB_IN), D_MODEL ** -0.5),
        'ssd_conv_w': jax.random.uniform(next(ks), (ne, SSD_CONV, SSD_CONV_DIM), F32, -0.5, 0.5),
        'ssd_conv_b': nrm((ne, SSD_CONV_DIM), 0.02),
        'ssd_dt_bias': ssd_dt + jnp.log(-jnp.expm1(-ssd_dt)),
        'ssd_a_log': jnp.log(jax.random.uniform(next(ks), (ne, SSD_HEADS), F32, 1.0, 16.0)),
        'ssd_d': 1.0 + nrm((ne, SSD_HEADS), 0.1),
        'ssd_norm': gain((ne, SSD_WIDTH)),
        's5_a_re': -0.5 * jnp.exp(nrm((ne, S5_GROUPS, S5_STATE), 0.02)),
        's5_a_im': math.pi * n_idx + nrm((ne, S5_GROUPS, S5_STATE), 0.01),
        's5_log_dt': log_uniform((ne, S5_GROUPS), 1e-3, 1e-1),
        's5_b_re': nrm((ne, S5_GROUPS, S5_STATE, S5_GROUP), (2 * S5_GROUP) ** -0.5),
        's5_b_im': nrm((ne, S5_GROUPS, S5_STATE, S5_GROUP), (2 * S5_GROUP) ** -0.5),
        's5_c_re': nrm((ne, S5_GROUPS, S5_GROUP, S5_STATE), S5_STATE ** -0.5),
        's5_c_im': nrm((ne, S5_GROUPS, S5_GROUP, S5_STATE), S5_STATE ** -0.5),
        's5_d': nrm((ne, S5_WIDTH), 1.0),
        's5_w_glu': nrm((ne, S5_WIDTH, S5_WIDTH), S5_WIDTH ** -0.5),
        'hyb_w_out': nrm((ne, HYB_OUT, D_MODEL), HYB_OUT ** -0.5),
        'mla_norm': gain((no, D_MODEL)),
        'mla_w_in': nrm((no, D_MODEL, MLA_IN), D_MODEL ** -0.5),
        'mla_q_norm': gain((no, MLA_Q_RANK)),
        'mla_w_uq': nrm((no, MLA_Q_RANK, MLA_HEADS * (MLA_NOPE + MLA_ROPE)), MLA_Q_RANK ** -0.5),
        'mla_kv_norm': gain((no, MLA_KV_RANK)),
        'mla_w_ukv': nrm((no, MLA_KV_RANK, MLA_HEADS * (MLA_NOPE + MLA_V)), MLA_KV_RANK ** -0.5),
        'mla_w_out': nrm((no, MLA_WIDTH, D_MODEL), MLA_WIDTH ** -0.5),
        'final_norm': gain((D_MODEL,)),
    }


def reference(x, meta, hyb_norm, hyb_w_in, ssd_conv_w, ssd_conv_b, ssd_dt_bias, ssd_a_log, ssd_d,
              ssd_norm, s5_a_re, s5_a_im, s5_log_dt, s5_b_re, s5_b_im, s5_c_re, s5_c_im, s5_d,
              s5_w_glu, hyb_w_out, mla_norm, mla_w_in, mla_q_norm, mla_w_uq, mla_kv_norm, mla_w_ukv,
              mla_w_out, final_norm):
    bsz = x.shape[0]
    h = jnp.concatenate([jnp.broadcast_to(meta.astype(x.dtype)[None], (bsz, N_META, D_MODEL)), x], axis=1)
    for layer in range(DEPTH):
        i = layer // 2
        if layer % 2 == 0:
            hn = rms_norm(h, hyb_norm[i])
            z, xbc, dt_raw, u, gate = jnp.split(hn @ hyb_w_in[i], HYB_SPLITS, axis=-1)
            y_a = ssd_branch(z, xbc, dt_raw, ssd_conv_w[i], ssd_conv_b[i], ssd_dt_bias[i],
                             ssd_a_log[i], ssd_d[i], ssd_norm[i])
            y_b = s5_branch(u, gate, s5_a_re[i], s5_a_im[i], s5_log_dt[i], s5_b_re[i], s5_b_im[i],
                            s5_c_re[i], s5_c_im[i], s5_d[i], s5_w_glu[i])
            h = h + jnp.concatenate([y_a, y_b], axis=-1) @ hyb_w_out[i]
        else:
            hn = rms_norm(h, mla_norm[i])
            y_c = mla_branch(hn, mla_w_in[i], mla_q_norm[i], mla_w_uq[i], mla_kv_norm[i], mla_w_ukv[i])
            h = h + y_c @ mla_w_out[i]
    return rms_norm(h[:, N_META:], final_norm)
```

```python
import functools
import math

import jax
import jax.numpy as jnp
from jax import lax
from jax.experimental import pallas as pl
from jax.experimental.pallas import tpu as pltpu

F32 = jnp.float32
BF16 = jnp.bfloat16

CHUNK = 64
N_META = 16
FRONT_PAD = (-N_META) % CHUNK
NORM_EPS = 1e-6

SSD_HEADDIM = 64
SSD_GROUPS = 8
SSD_STATE = 128
SSD_CONV = 4

S5_GROUP = 16
S5_STATE = 64
S5_BLOCK = 16
S5_GB = 8

MLA_NOPE = 128
MLA_ROPE = 64
MLA_V = 128
MLA_QK_PAD = 256
ROPE_BASE = 10000.0

ROW_ALIGN = 1280
MM_ROWS = 640
NORM_ROWS = 256
SSD_ROWS = 128
ATT_ROWS = 256
NEG = -1e30
VMEM_LIMIT = 56 * 1024 * 1024

_NT = (((1,), (1,)), ((), ()))


def _cparams(*sem):
    return pltpu.CompilerParams(dimension_semantics=sem, vmem_limit_bytes=VMEM_LIMIT)


def _silu(x):
    return x * (1.0 / (1.0 + jnp.exp(-x)))


def _sigmoid(x):
    return 1.0 / (1.0 + jnp.exp(-x))


def _softplus(x):
    return jnp.maximum(x, 0.0) + jnp.log(1.0 + jnp.exp(-jnp.abs(x)))


def _pick(n, cands):
    for c in cands:
        if n % c == 0:
            return c
    raise ValueError(f"no tile for {n} in {cands}")


def _rms_kernel(x_ref, g_ref, o_ref):
    x = x_ref[...]
    ms = jnp.mean(x * x, axis=-1, keepdims=True)
    o_ref[...] = (x * lax.rsqrt(ms + NORM_EPS) * g_ref[...]).astype(o_ref.dtype)


def rms_norm_rows(x, g, out_dtype, rows, row_block_offset=0, n_rows=None):
    n, d = x.shape
    n_rows = n if n_rows is None else n_rows
    return pl.pallas_call(
        _rms_kernel,
        grid=(n_rows // rows,),
        in_specs=[pl.BlockSpec((rows, d), lambda i: (i + row_block_offset, 0)),
                  pl.BlockSpec((1, d), lambda i: (0, 0))],
        out_specs=pl.BlockSpec((rows, d), lambda i: (i, 0)),
        out_shape=jax.ShapeDtypeStruct((n_rows, d), out_dtype),
        compiler_params=_cparams("parallel"),
        name="rms_norm",
    )(x, g.reshape(1, d).astype(F32))


def _mm_kernel(*refs, nx, has_res):
    o_ref = refs[-1]
    acc = None
    for xr, wr in zip(refs[:nx], refs[nx:2 * nx]):
        d = jnp.dot(xr[...], wr[...], preferred_element_type=F32)
        acc = d if acc is None else acc + d
    if has_res:
        acc = acc + refs[2 * nx][...]
    o_ref[...] = acc.astype(o_ref.dtype)


def matmul(xs, ws, out_dtype, res=None, name="matmul"):
    m = xs[0].shape[0]
    n = ws[0].shape[1]
    tm = _pick(m, (MM_ROWS,))
    tn = _pick(n, (512, 256, 128))
    in_specs = [pl.BlockSpec((tm, x.shape[1]), lambda i, j: (i, 0)) for x in xs]
    in_specs += [pl.BlockSpec((w.shape[0], tn), lambda i, j: (0, j)) for w in ws]
    args = list(xs) + list(ws)
    if res is not None:
        in_specs.append(pl.BlockSpec((tm, tn), lambda i, j: (i, j)))
        args.append(res)
    return pl.pallas_call(
        functools.partial(_mm_kernel, nx=len(xs), has_res=res is not None),
        grid=(m // tm, n // tn),
        in_specs=in_specs,
        out_specs=pl.BlockSpec((tm, tn), lambda i, j: (i, j)),
        out_shape=jax.ShapeDtypeStruct((m, n), out_dtype),
        compiler_params=_cparams("parallel", "parallel"),
        name=name,
    )(*args)


def _ssd_kernel(z_ref, x_ref, b_ref, c_ref, dt_ref, cwx_ref, cwb_ref, cwc_ref, cbx_ref, cbb_ref,
                cbc_ref, dtb_ref, alog_ref, dskip_ref, ng_ref, o_ref, ext_ref, state_ref, *, T, E):
    c = pl.program_id(1)
    W = E * SSD_HEADDIM
    N = SSD_STATE

    @pl.when(c == 0)
    def _():
        ext_ref[0:8, :] = jnp.zeros((8, W + 2 * N), F32)
        state_ref[...] = jnp.zeros_like(state_ref)

    @pl.when(c > 0)
    def _():
        ext_ref[0:8, :] = ext_ref[T:T + 8, :]

    ext_ref[8:T + 8, 0:W] = x_ref[...]
    ext_ref[8:T + 8, W:W + N] = b_ref[...]
    ext_ref[8:T + 8, W + N:W + 2 * N] = c_ref[...]

    cw = jnp.concatenate([cwx_ref[...], cwb_ref[...], cwc_ref[...]], axis=1)
    acc = jnp.concatenate([cbx_ref[...], cbb_ref[...], cbc_ref[...]], axis=1)
    for k in range(SSD_CONV):
        acc = acc + cw[k:k + 1, :] * ext_ref[pl.ds(8 - (SSD_CONV - 1) + k, T), :]
    xc = _silu(acc)
    xs = xc[:, 0:W]
    bm = xc[:, W:W + N]
    cm = xc[:, W + N:W + 2 * N]

    rows = c * T + lax.broadcasted_iota(jnp.int32, (T, 128), 0)
    dtv = _softplus(dt_ref[...] + dtb_ref[...])
    dtv = jnp.where(rows >= FRONT_PAD, dtv, 0.0)
    a = -jnp.exp(alog_ref[...])
    da = dtv * a
    r_i = lax.broadcasted_iota(jnp.int32, (T, T), 0)
    c_i = lax.broadcasted_iota(jnp.int32, (T, T), 1)
    tril = r_i >= c_i
    acum = jnp.dot(tril.astype(F32), da, precision=lax.Precision.HIGHEST,
                   preferred_element_type=F32)
    acum_t = acum.T

    lane_t = lax.broadcasted_iota(jnp.int32, (T, 128), 1)
    lane_1 = lax.broadcasted_iota(jnp.int32, (1, 128), 1)

    def expand(v, lane):
        parts = [jnp.where(lane < SSD_HEADDIM, v[:, 2 * k:2 * k + 1], v[:, 2 * k + 1:2 * k + 2])
                 for k in range(E // 2)]
        return parts[0] if len(parts) == 1 else jnp.concatenate(parts, axis=1)

    xdt = xs * expand(dtv, lane_t)
    cb = lax.dot_general(cm.astype(BF16), bm.astype(BF16), _NT, preferred_element_type=F32)
    yd = []
    for e in range(E):
        seg = acum[:, e:e + 1] - acum_t[e:e + 1, :]
        g = (cb * jnp.exp(jnp.where(tril, seg, NEG))).astype(BF16)
        yd.append(jnp.dot(g, xdt[:, e * SSD_HEADDIM:(e + 1) * SSD_HEADDIM].astype(BF16),
                          preferred_element_type=F32))
    y_diag = jnp.concatenate(yd, axis=1)

    a_last = acum[T - 1:T, :]
    h_t = state_ref[...]
    y_off = jnp.dot(cm.astype(BF16), h_t.astype(BF16), preferred_element_type=F32)
    y_off = y_off * expand(jnp.exp(acum), lane_t)
    xw = (xdt * expand(jnp.exp(a_last - acum), lane_t)).astype(BF16)
    state_ref[...] = h_t * expand(jnp.exp(a_last), lane_1) + jnp.dot(
        bm.T.astype(BF16), xw, preferred_element_type=F32)

    y = y_diag + y_off + dskip_ref[...] * xs
    y = y * _silu(z_ref[...])
    ms = jnp.mean(y * y, axis=-1, keepdims=True)
    o_ref[...] = (y * lax.rsqrt(ms + NORM_EPS) * ng_ref[...]).astype(o_ref.dtype)


def ssd_mixer(proj, dtg, conv_w, conv_b, dt_bias, a_log, d_skip, norm_g, d_model):
    lp = proj.shape[0]
    d = d_model
    heads = d // SSD_HEADDIM
    e = heads // SSD_GROUPS
    w = d // SSD_GROUPS
    n = SSD_STATE
    t = SSD_ROWS
    xb, bb, cb = d // w, 2 * d // n, (2 * d + SSD_GROUPS * n) // n

    def padg(v):
        return jnp.pad(v.astype(F32).reshape(SSD_GROUPS, e), ((0, 0), (0, 128 - e))).reshape(1, -1)

    cw = conv_w.astype(F32)
    cbias = conv_b.astype(F32).reshape(1, -1)
    dsk = jnp.repeat(d_skip.astype(F32), SSD_HEADDIM).reshape(1, d)
    return pl.pallas_call(
        functools.partial(_ssd_kernel, T=t, E=e),
        grid=(SSD_GROUPS, lp // t),
        in_specs=[
            pl.BlockSpec((t, w), lambda g, c: (c, g)),
            pl.BlockSpec((t, w), lambda g, c: (c, xb + g)),
            pl.BlockSpec((t, n), lambda g, c: (c, bb + g)),
            pl.BlockSpec((t, n), lambda g, c: (c, cb + g)),
            pl.BlockSpec((t, 128), lambda g, c: (c, g)),
            pl.BlockSpec((SSD_CONV, w), lambda g, c: (0, g)),
            pl.BlockSpec((SSD_CONV, n), lambda g, c: (0, d // n + g)),
            pl.BlockSpec((SSD_CONV, n), lambda g, c: (0, (d + SSD_GROUPS * n) // n + g)),
            pl.BlockSpec((1, w), lambda g, c: (0, g)),
            pl.BlockSpec((1, n), lambda g, c: (0, d // n + g)),
            pl.BlockSpec((1, n), lambda g, c: (0, (d + SSD_GROUPS * n) // n + g)),
            pl.BlockSpec((1, 128), lambda g, c: (0, g)),
            pl.BlockSpec((1, 128), lambda g, c: (0, g)),
            pl.BlockSpec((1, w), lambda g, c: (0, g)),
            pl.BlockSpec((1, w), lambda g, c: (0, g)),
        ],
        out_specs=pl.BlockSpec((t, w), lambda g, c: (c, g)),
        out_shape=jax.ShapeDtypeStruct((lp, d), BF16),
        scratch_shapes=[pltpu.VMEM((t + 8, w + 2 * n), F32), pltpu.VMEM((n, w), F32)],
        compiler_params=_cparams("parallel", "arbitrary"),
        name="ssd_mixer",
    )(proj, proj, proj, proj, dtg, cw, cw, cw, cbias, cbias, cbias, padg(dt_bias), padg(a_log),
      dsk, norm_g.astype(F32).reshape(1, d))


def _s5_kernel(u_ref, m_ref, bs_ref, cs_ref, a1_ref, a2_ref, y_ref, ins_ref, int_ref, sp_ref, st_ref,
               *, GB, NBT):
    j = pl.program_id(1)

    @pl.when(j == 0)
    def _():
        st_ref[...] = jnp.zeros_like(st_ref)

    for k in range(GB):
        inp = jnp.dot(u_ref[k], bs_ref[k], preferred_element_type=F32)
        ins_ref[k * NBT:(k + 1) * NBT, :] = inp[:, 0:128]
        int_ref[k * NBT:(k + 1) * NBT, :] = inp[:, 128:256]

    a1 = a1_ref[...]
    a2 = a2_ref[...]

    def body(b, carry):
        s, t = carry
        rows = pl.ds(b, GB, stride=NBT)
        sp_ref[rows, :] = s
        return (a1 * s + a2 * t + ins_ref[rows, :], a1 * t - a2 * s + int_ref[rows, :])

    s, t = lax.fori_loop(0, NBT, body, (st_ref[0], st_ref[1]))
    st_ref[0] = s
    st_ref[1] = t

    for k in range(GB):
        y_ref[k] = (jnp.dot(u_ref[k], m_ref[k], preferred_element_type=F32)
                    + jnp.dot(sp_ref[k * NBT:(k + 1) * NBT, :].astype(BF16), cs_ref[k],
                              preferred_element_type=F32))


def _s5_operators(a_re, a_im, log_dt, b_re, b_im, c_re, c_im):
    hi = lax.Precision.HIGHEST
    g, p = a_re.shape
    kb = S5_BLOCK
    a_re, a_im = a_re.astype(F32), a_im.astype(F32)
    b_re, b_im = b_re.astype(F32), b_im.astype(F32)
    c_re, c_im = c_re.astype(F32), c_im.astype(F32)
    dt = jnp.exp(log_dt.astype(F32))[:, None]
    mag = jnp.exp(dt * a_re)
    ab_re, ab_im = mag * jnp.cos(dt * a_im), mag * jnp.sin(dt * a_im)
    den = a_re * a_re + a_im * a_im
    k_re = ((ab_re - 1.0) * a_re + ab_im * a_im) / den
    k_im = (ab_im * a_re - (ab_re - 1.0) * a_im) / den
    bb_re = k_re[..., None] * b_re - k_im[..., None] * b_im
    bb_im = k_re[..., None] * b_im + k_im[..., None] * b_re
    ks = jnp.arange(kb + 1, dtype=F32)[:, None, None]
    pm = jnp.exp(ks * (dt * a_re))
    pr, pi = pm * jnp.cos(ks * (dt * a_im)), pm * jnp.sin(ks * (dt * a_im))
    abr = pr[..., None] * bb_re - pi[..., None] * bb_im
    abi = pr[..., None] * bb_im + pi[..., None] * bb_re
    mk = (jnp.einsum('gjp,kgpi->kgji', c_re, abr[:kb], precision=hi)
          - jnp.einsum('gjp,kgpi->kgji', c_im, abi[:kb], precision=hi))
    tt = jnp.arange(kb)
    lag = tt[None, :] - tt[:, None]
    mfull = jnp.where((lag >= 0)[:, :, None, None, None], mk[jnp.clip(lag, 0)], 0.0)
    m = mfull.transpose(2, 0, 4, 1, 3).reshape(g, kb * S5_GROUP, kb * S5_GROUP)
    bs_re = abr[:kb][::-1].transpose(1, 0, 3, 2).reshape(g, kb * S5_GROUP, p)
    bs_im = abi[:kb][::-1].transpose(1, 0, 3, 2).reshape(g, kb * S5_GROUP, p)
    bs = jnp.concatenate([bs_re, bs_im, bs_im, bs_re], axis=-1)
    car = c_re[None] * pr[1:, :, None, :] - c_im[None] * pi[1:, :, None, :]
    cai = c_re[None] * pi[1:, :, None, :] + c_im[None] * pr[1:, :, None, :]
    cs = jnp.concatenate([car.transpose(1, 3, 0, 2), -cai.transpose(1, 3, 0, 2)], axis=1)
    cs = cs.reshape(g, 2 * p, kb * S5_GROUP)
    a1 = jnp.concatenate([pr[kb], pr[kb]], axis=-1)
    a2 = jnp.concatenate([-pi[kb], pi[kb]], axis=-1)
    return m.astype(BF16), bs.astype(BF16), cs.astype(BF16), a1, a2


def s5_ssm(u_blk, ops):
    m, bs, cs, a1, a2 = ops
    g, nb, kw = u_blk.shape
    gb = S5_GB
    nbt = nb // 2
    return pl.pallas_call(
        functools.partial(_s5_kernel, GB=gb, NBT=nbt),
        grid=(g // gb, nb // nbt),
        in_specs=[
            pl.BlockSpec((gb, nbt, kw), lambda i, j: (i, j, 0)),
            pl.BlockSpec((gb, kw, kw), lambda i, j: (i, 0, 0)),
            pl.BlockSpec((gb, kw, kw), lambda i, j: (i, 0, 0)),
            pl.BlockSpec((gb, 2 * S5_STATE, kw), lambda i, j: (i, 0, 0)),
            pl.BlockSpec((gb, 2 * S5_STATE), lambda i, j: (i, 0)),
            pl.BlockSpec((gb, 2 * S5_STATE), lambda i, j: (i, 0)),
        ],
        out_specs=pl.BlockSpec((gb, nbt, kw), lambda i, j: (i, j, 0)),
        out_shape=jax.ShapeDtypeStruct((g, nb, kw), F32),
        scratch_shapes=[pltpu.VMEM((gb * nbt, 2 * S5_STATE), F32), pltpu.VMEM((gb * nbt, 2 * S5_STATE), F32),
                        pltpu.VMEM((gb * nbt, 2 * S5_STATE), F32), pltpu.VMEM((2, gb, 2 * S5_STATE), F32)],
        compiler_params=_cparams("parallel", "arbitrary"),
        name="s5_ssm",
    )(u_blk, m, bs, cs, a1, a2)


def _glu_kernel(y_ref, u_ref, gate_ref, d_ref, w_ref, o_ref):
    y = y_ref[...] + d_ref[...] * u_ref[...]
    c0 = math.sqrt(2.0 / math.pi)
    g = 0.5 * y * (1.0 + jnp.tanh(c0 * (y + 0.044715 * (y * y * y))))
    lin = jnp.dot(g.astype(BF16), w_ref[...], preferred_element_type=F32)
    o_ref[...] = (g * _sigmoid(lin) * _silu(gate_ref[...])).astype(o_ref.dtype)


def s5_glu(y_ssm, proj, d_skip, w_glu, u_off, gate_off):
    lp, sw = y_ssm.shape
    tm = _pick(lp, (MM_ROWS,))
    return pl.pallas_call(
        _glu_kernel,
        grid=(lp // tm,),
        in_specs=[
            pl.BlockSpec((tm, sw), lambda i: (i, 0)),
            pl.BlockSpec((tm, sw), lambda i: (i, u_off // sw)),
            pl.BlockSpec((tm, sw), lambda i: (i, gate_off // sw)),
            pl.BlockSpec((1, sw), lambda i: (0, 0)),
            pl.BlockSpec((sw, sw), lambda i: (0, 0)),
        ],
        out_specs=pl.BlockSpec((tm, sw), lambda i: (i, 0)),
        out_shape=jax.ShapeDtypeStruct((lp, sw), BF16),
        compiler_params=_cparams("parallel"),
        name="s5_glu",
    )(y_ssm, proj, proj, d_skip.astype(F32).reshape(1, sw), w_glu.astype(BF16))


def _rope(r, cos, sin):
    half = MLA_ROPE // 2
    x1, x2 = r[:, :half], r[:, half:]
    return jnp.concatenate([x1 * cos - x2 * sin, x2 * cos + x1 * sin], axis=1)


def _q_kernel(cq_ref, g_ref, w_ref, cos_ref, sin_ref, o_ref, xn_ref, *, scale):
    @pl.when(pl.program_id(1) == 0)
    def _():
        x = cq_ref[...]
        ms = jnp.mean(x * x, axis=-1, keepdims=True)
        xn_ref[...] = (x * lax.rsqrt(ms + NORM_EPS) * g_ref[...]).astype(BF16)

    acc = jnp.dot(xn_ref[...], w_ref[0], preferred_element_type=F32)
    tm = acc.shape[0]
    pad = jnp.zeros((tm, MLA_QK_PAD - MLA_NOPE - MLA_ROPE), F32)
    for hh in range(2):
        qn = acc[:, hh * MLA_NOPE:(hh + 1) * MLA_NOPE]
        r0 = 2 * MLA_NOPE + hh * MLA_ROPE
        qr = _rope(acc[:, r0:r0 + MLA_ROPE], cos_ref[...], sin_ref[...])
        o_ref[hh] = (jnp.concatenate([qn, qr, pad], axis=1) * scale).astype(o_ref.dtype)


def _kv_kernel(ckv_ref, g_ref, wk_ref, wvt_ref, kr_ref, cos_ref, sin_ref, k_ref, vt_ref, xn_ref):
    @pl.when(pl.program_id(1) == 0)
    def _():
        x = ckv_ref[...]
        ms = jnp.mean(x * x, axis=-1, keepdims=True)
        xn_ref[...] = (x * lax.rsqrt(ms + NORM_EPS) * g_ref[...]).astype(BF16)

    xn = xn_ref[...]
    k = jnp.dot(xn, wk_ref[0], preferred_element_type=F32)
    tm = k.shape[0]
    kr = _rope(kr_ref[...][:, 0:MLA_ROPE], cos_ref[...], sin_ref[...])
    pad = jnp.zeros((tm, MLA_QK_PAD - MLA_NOPE - MLA_ROPE), F32)
    vt = lax.dot_general(wvt_ref[0], xn, _NT, preferred_element_type=F32)
    for hh in range(2):
        kn = k[:, hh * MLA_NOPE:(hh + 1) * MLA_NOPE]
        k_ref[hh, 0] = jnp.concatenate([kn, kr, pad], axis=1).astype(k_ref.dtype)
        vt_ref[hh, 0] = vt[hh * MLA_V:(hh + 1) * MLA_V, :].astype(vt_ref.dtype)


def _attn_kernel(q_ref, k_ref, vt_ref, gate_ref, o_ref, *, TQ):
    i = pl.program_id(1)
    q = q_ref[0]
    k_row = lax.broadcasted_iota(jnp.int32, (TQ, TQ), 0)
    q_col = lax.broadcasted_iota(jnp.int32, (TQ, TQ), 1)

    def tile(j, carry, masked):
        m, l, acc = carry
        s = lax.dot_general(k_ref[0, j], q, _NT, preferred_element_type=F32)
        if masked:
            k_pos = j * TQ + k_row
            ok = (k_pos // CHUNK <= (i * TQ + q_col) // CHUNK) & (k_pos >= FRONT_PAD)
            s = jnp.where(ok, s, NEG)
        m_new = jnp.maximum(m, jnp.max(s, axis=0, keepdims=True))
        alpha = jnp.exp(m - m_new)
        p = jnp.exp(s - m_new)
        l = alpha * l + jnp.sum(p, axis=0, keepdims=True)
        acc = alpha * acc + jnp.dot(vt_ref[0, j], p.astype(BF16), preferred_element_type=F32)
        return m_new, l, acc

    carry = (jnp.full((1, TQ), NEG, F32), jnp.zeros((1, TQ), F32), jnp.zeros((MLA_V, TQ), F32))
    carry = tile(0, carry, True)
    carry = lax.fori_loop(1, i, lambda j, c: tile(j, c, False), carry)
    carry = lax.fori_loop(jnp.maximum(i, 1), i + 1, lambda j, c: tile(j, c, True), carry)
    _, l, acc = carry
    o = (acc * (1.0 / l)).T
    o_ref[...] = (o * _silu(gate_ref[...])).astype(o_ref.dtype)


def mla_mixer(proj, kr_raw, q_norm, w_uq, kv_norm, w_ukv, d_model, row_pos):
    lp = proj.shape[0]
    d = d_model
    h = d // MLA_V
    qr, kvr = d // 4, d // 8
    ta = ATT_ROWS
    nt = lp // ta
    half = MLA_ROPE // 2
    inv_freq = ROPE_BASE ** (-jnp.arange(0, MLA_ROPE, 2, dtype=F32) / MLA_ROPE)
    ang = row_pos[:, None] * inv_freq[None, :]
    cos, sin = jnp.cos(ang), jnp.sin(ang)

    wq = w_uq.reshape(qr, h // 2, 2, MLA_NOPE + MLA_ROPE)
    wq = jnp.concatenate([wq[..., :MLA_NOPE].reshape(qr, h // 2, 2 * MLA_NOPE),
                          wq[..., MLA_NOPE:].reshape(qr, h // 2, 2 * MLA_ROPE)], axis=-1)
    wq = wq.transpose(1, 0, 2).astype(BF16)
    wkv = w_ukv.reshape(kvr, h // 2, 2, MLA_NOPE + MLA_V)
    wk = wkv[..., :MLA_NOPE].reshape(kvr, h // 2, 2 * MLA_NOPE).transpose(1, 0, 2).astype(BF16)
    wvt = wkv[..., MLA_NOPE:].reshape(kvr, h // 2, 2 * MLA_V).transpose(1, 2, 0).astype(BF16)

    tmq = _pick(lp, (MM_ROWS,))
    scale = (MLA_NOPE + MLA_ROPE) ** -0.5
    q3 = pl.pallas_call(
        functools.partial(_q_kernel, scale=scale),
        grid=(lp // tmq, h // 2),
        in_specs=[
            pl.BlockSpec((tmq, qr), lambda i, j: (i, 0)),
            pl.BlockSpec((1, qr), lambda i, j: (0, 0)),
            pl.BlockSpec((1, qr, 2 * (MLA_NOPE + MLA_ROPE)), lambda i, j: (j, 0, 0)),
            pl.BlockSpec((tmq, half), lambda i, j: (i, 0)),
            pl.BlockSpec((tmq, half), lambda i, j: (i, 0)),
        ],
        out_specs=pl.BlockSpec((2, tmq, MLA_QK_PAD), lambda i, j: (j, i, 0)),
        out_shape=jax.ShapeDtypeStruct((h, lp, MLA_QK_PAD), BF16),
        scratch_shapes=[pltpu.VMEM((tmq, qr), BF16)],
        compiler_params=_cparams("parallel", "arbitrary"),
        name="mla_q",
    )(proj, q_norm.astype(F32).reshape(1, qr), wq, cos, sin)

    k4, vt4 = pl.pallas_call(
        _kv_kernel,
        grid=(nt, h // 2),
        in_specs=[
            pl.BlockSpec((ta, kvr), lambda i, j: (i, qr // kvr)),
            pl.BlockSpec((1, kvr), lambda i, j: (0, 0)),
            pl.BlockSpec((1, kvr, 2 * MLA_NOPE), lambda i, j: (j, 0, 0)),
            pl.BlockSpec((1, 2 * MLA_V, kvr), lambda i, j: (j, 0, 0)),
            pl.BlockSpec((ta, 128), lambda i, j: (i, 0)),
            pl.BlockSpec((ta, half), lambda i, j: (i, 0)),
            pl.BlockSpec((ta, half), lambda i, j: (i, 0)),
        ],
        out_specs=[pl.BlockSpec((2, 1, ta, MLA_QK_PAD), lambda i, j: (j, i, 0, 0)),
                   pl.BlockSpec((2, 1, MLA_V, ta), lambda i, j: (j, i, 0, 0))],
        out_shape=[jax.ShapeDtypeStruct((h, nt, ta, MLA_QK_PAD), BF16),
                   jax.ShapeDtypeStruct((h, nt, MLA_V, ta), BF16)],
        scratch_shapes=[pltpu.VMEM((ta, kvr), BF16)],
        compiler_params=_cparams("parallel", "arbitrary"),
        name="mla_kv",
    )(proj, kv_norm.astype(F32).reshape(1, kvr), wk, wvt, kr_raw, cos, sin)

    gate_blk = (qr + kvr) // MLA_V
    return pl.pallas_call(
        functools.partial(_attn_kernel, TQ=ta),
        grid=(h, nt),
        in_specs=[
            pl.BlockSpec((1, ta, MLA_QK_PAD), lambda hh, i: (hh, i, 0)),
            pl.BlockSpec((1, nt, ta, MLA_QK_PAD), lambda hh, i: (hh, 0, 0, 0)),
            pl.BlockSpec((1, nt, MLA_V, ta), lambda hh, i: (hh, 0, 0, 0)),
            pl.BlockSpec((ta, MLA_V), lambda hh, i: (i, gate_blk + hh)),
        ],
        out_specs=pl.BlockSpec((ta, MLA_V), lambda hh, i: (i, hh)),
        out_shape=jax.ShapeDtypeStruct((lp, d), BF16),
        compiler_params=_cparams("parallel", "parallel"),
        name="mla_attention",
    )(q3, k4, vt4, proj)


def _hybrid_layer(h, norm_g, w_in, conv_w, conv_b, dt_bias, a_log, ssd_d, ssd_norm, s5_ops, s5_d,
                  w_glu, w_out):
    lp, d = h.shape
    heads = d // SSD_HEADDIM
    e = heads // SSD_GROUPS
    conv_dim = d + 2 * SSD_GROUPS * SSD_STATE
    s5w = d // 2
    o_dt = d + conv_dim
    hn = rms_norm_rows(h, norm_g, BF16, NORM_ROWS)
    w_main = jnp.concatenate([w_in[:, :o_dt], w_in[:, o_dt + heads:]], axis=1).astype(BF16)
    w_dt = w_in[:, o_dt:o_dt + heads].reshape(d, SSD_GROUPS, e)
    w_dt = jnp.pad(w_dt, ((0, 0), (0, 0), (0, 128 // SSD_GROUPS - e))).reshape(d, 128).astype(BF16)
    proj = matmul([hn], [w_main], F32, name="hyb_in_proj")
    dt_raw = matmul([hn], [w_dt], F32, name="hyb_dt_proj")
    dtg = jnp.pad(dt_raw.reshape(lp, SSD_GROUPS, 128 // SSD_GROUPS),
                  ((0, 0), (0, 0), (0, 128 - 128 // SSD_GROUPS))).reshape(lp, SSD_GROUPS * 128)
    y_a = ssd_mixer(proj, dtg, conv_w, conv_b, dt_bias, a_log, ssd_d, ssd_norm, d)

    u_off = o_dt
    gate_off = o_dt + s5w
    g5 = s5w // S5_GROUP
    nb = lp // S5_BLOCK
    u_blk = proj[:, u_off:u_off + s5w].astype(BF16).reshape(nb, S5_BLOCK, g5, S5_GROUP)
    u_blk = u_blk.transpose(2, 0, 1, 3).reshape(g5, nb, S5_BLOCK * S5_GROUP)
    y_blk = s5_ssm(u_blk, s5_ops)
    y_ssm = y_blk.reshape(g5, nb, S5_BLOCK, S5_GROUP).transpose(1, 2, 0, 3).reshape(lp, s5w)
    y_b = s5_glu(y_ssm, proj, s5_d, w_glu, u_off, gate_off)
    w_o = w_out.astype(BF16)
    return matmul([y_a, y_b], [w_o[:d], w_o[d:]], F32, res=h, name="hyb_out_proj")


def _mla_layer(h, norm_g, w_in, q_norm, w_uq, kv_norm, w_ukv, w_out, row_pos):
    lp, d = h.shape
    qr, kvr = d // 4, d // 8
    hn = rms_norm_rows(h, norm_g, BF16, NORM_ROWS)
    o_kr = qr + kvr
    w_main = jnp.concatenate([w_in[:, :o_kr], w_in[:, o_kr + MLA_ROPE:]], axis=1).astype(BF16)
    w_kr = jnp.pad(w_in[:, o_kr:o_kr + MLA_ROPE], ((0, 0), (0, 128 - MLA_ROPE))).astype(BF16)
    proj = matmul([hn], [w_main], F32, name="mla_in_proj")
    kr_raw = matmul([hn], [w_kr], F32, name="mla_kr_proj")
    y_c = mla_mixer(proj, kr_raw, q_norm, w_uq, kv_norm, w_ukv, d, row_pos)
    return matmul([y_c], [w_out.astype(BF16)], F32, res=h, name="mla_out_proj")


def kernel(x, meta, hyb_norm, hyb_w_in, ssd_conv_w, ssd_conv_b, ssd_dt_bias, ssd_a_log, ssd_d, ssd_norm, s5_a_re, s5_a_im, s5_log_dt, s5_b_re, s5_b_im, s5_c_re, s5_c_im, s5_d, s5_w_glu, hyb_w_out, mla_norm, mla_w_in, mla_q_norm, mla_w_uq, mla_kv_norm, mla_w_ukv, mla_w_out, final_norm):
    bsz, seq, d = x.shape
    assert bsz == 1 and seq % CHUNK == 0 and meta.shape == (N_META, d)
    depth = hyb_norm.shape[0] + mla_norm.shape[0]
    first = FRONT_PAD + N_META
    lp = -(-(first + seq) // ROW_ALIGN) * ROW_ALIGN
    h = jnp.concatenate([jnp.zeros((FRONT_PAD, d), F32), meta.astype(F32), x[0].astype(F32),
                         jnp.zeros((lp - first - seq, d), F32)], axis=0)
    row_pos = jnp.arange(lp, dtype=jnp.int32).astype(F32) - float(FRONT_PAD)
    for layer in range(depth):
        i = layer // 2
        if layer % 2 == 0:
            ops = _s5_operators(s5_a_re[i], s5_a_im[i], s5_log_dt[i], s5_b_re[i], s5_b_im[i],
                                s5_c_re[i], s5_c_im[i])
            h = _hybrid_layer(h, hyb_norm[i], hyb_w_in[i], ssd_conv_w[i], ssd_conv_b[i],
                              ssd_dt_bias[i], ssd_a_log[i], ssd_d[i], ssd_norm[i], ops, s5_d[i],
                              s5_w_glu[i], hyb_w_out[i])
        else:
            h = _mla_layer(h, mla_norm[i], mla_w_in[i], mla_q_norm[i], mla_w_uq[i], mla_kv_norm[i],
                           mla_w_ukv[i], mla_w_out[i], row_pos)
    out = rms_norm_rows(h, final_norm, x.dtype, CHUNK, row_block_offset=first // CHUNK, n_rows=seq)
    return out.reshape(1, seq, d)
```

```python
import functools
import math

import jax
import jax.numpy as jnp
from jax import lax
from jax.experimental import pallas as pl
from jax.experimental.pallas import tpu as pltpu

F32 = jnp.float32
BF16 = jnp.bfloat16

CHUNK = 64
N_META = 16
FRONT_PAD = (-N_META) % CHUNK
NORM_EPS = 1e-6

SSD_HEADDIM = 64
SSD_GROUPS = 8
SSD_STATE = 128
SSD_CONV = 4

S5_GROUP = 16
S5_STATE = 64
S5_BLOCK = 16
S5_GB = 8

MLA_NOPE = 128
MLA_ROPE = 64
MLA_V = 128
MLA_QK_PAD = 256
ROPE_BASE = 10000.0

ROW_ALIGN = 1280
MM_ROWS = 640
NORM_ROWS = 256
SSD_ROWS = 128
ATT_ROWS = 256
ATT_Q_TILES = 5
NEG = -1e30
VMEM_LIMIT = 56 * 1024 * 1024

_NT = (((1,), (1,)), ((), ()))


def _cparams(*sem):
    return pltpu.CompilerParams(dimension_semantics=sem, vmem_limit_bytes=VMEM_LIMIT)


def _silu(x):
    return x * (1.0 / (1.0 + jnp.exp(-x)))


def _sigmoid(x):
    return 1.0 / (1.0 + jnp.exp(-x))


def _softplus(x):
    return jnp.maximum(x, 0.0) + jnp.log(1.0 + jnp.exp(-jnp.abs(x)))


def _pick(n, cands):
    for c in cands:
        if n % c == 0:
            return c
    raise ValueError(f"no tile for {n} in {cands}")


def _rms_kernel(x_ref, g_ref, o_ref):
    x = x_ref[...]
    ms = jnp.mean(x * x, axis=-1, keepdims=True)
    o_ref[...] = (x * lax.rsqrt(ms + NORM_EPS) * g_ref[...]).astype(o_ref.dtype)


def rms_norm_rows(x, g, out_dtype, rows, row_block_offset=0, n_rows=None):
    n, d = x.shape
    n_rows = n if n_rows is None else n_rows
    return pl.pallas_call(
        _rms_kernel,
        grid=(n_rows // rows,),
        in_specs=[pl.BlockSpec((rows, d), lambda i: (i + row_block_offset, 0)),
                  pl.BlockSpec((1, d), lambda i: (0, 0))],
        out_specs=pl.BlockSpec((rows, d), lambda i: (i, 0)),
        out_shape=jax.ShapeDtypeStruct((n_rows, d), out_dtype),
        compiler_params=_cparams("parallel"),
        name="rms_norm",
    )(x, g.reshape(1, d).astype(F32))


def _mm_kernel(*refs, nx, has_res):
    o_ref = refs[-1]
    acc = None
    for xr, wr in zip(refs[:nx], refs[nx:2 * nx]):
        d = jnp.dot(xr[...], wr[...], preferred_element_type=F32)
        acc = d if acc is None else acc + d
    if has_res:
        acc = acc + refs[2 * nx][...]
    o_ref[...] = acc.astype(o_ref.dtype)


def matmul(xs, ws, out_dtype, res=None, name="matmul"):
    m = xs[0].shape[0]
    n = ws[0].shape[1]
    tm = _pick(m, (MM_ROWS,))
    tn = _pick(n, (512, 256, 128))
    in_specs = [pl.BlockSpec((tm, x.shape[1]), lambda i, j: (i, 0)) for x in xs]
    in_specs += [pl.BlockSpec((w.shape[0], tn), lambda i, j: (0, j)) for w in ws]
    args = list(xs) + list(ws)
    if res is not None:
        in_specs.append(pl.BlockSpec((tm, tn), lambda i, j: (i, j)))
        args.append(res)
    return pl.pallas_call(
        functools.partial(_mm_kernel, nx=len(xs), has_res=res is not None),
        grid=(m // tm, n // tn),
        in_specs=in_specs,
        out_specs=pl.BlockSpec((tm, tn), lambda i, j: (i, j)),
        out_shape=jax.ShapeDtypeStruct((m, n), out_dtype),
        compiler_params=_cparams("parallel", "parallel"),
        name=name,
    )(*args)


def _ssd_kernel(z_ref, x_ref, b_ref, c_ref, dt_ref, cwx_ref, cwb_ref, cwc_ref, cbx_ref, cbb_ref,
                cbc_ref, dtb_ref, alog_ref, dskip_ref, ng_ref, o_ref, ext_ref, state_ref, *, T, E):
    c = pl.program_id(1)
    W = E * SSD_HEADDIM
    N = SSD_STATE

    @pl.when(c == 0)
    def _():
        ext_ref[0:8, :] = jnp.zeros((8, W + 2 * N), F32)
        state_ref[...] = jnp.zeros_like(state_ref)

    @pl.when(c > 0)
    def _():
        ext_ref[0:8, :] = ext_ref[T:T + 8, :]

    ext_ref[8:T + 8, 0:W] = x_ref[...]
    ext_ref[8:T + 8, W:W + N] = b_ref[...]
    ext_ref[8:T + 8, W + N:W + 2 * N] = c_ref[...]

    cw = jnp.concatenate([cwx_ref[...], cwb_ref[...], cwc_ref[...]], axis=1)
    acc = jnp.concatenate([cbx_ref[...], cbb_ref[...], cbc_ref[...]], axis=1)
    for k in range(SSD_CONV):
        acc = acc + cw[k:k + 1, :] * ext_ref[pl.ds(8 - (SSD_CONV - 1) + k, T), :]
    xc = _silu(acc)
    xs = xc[:, 0:W]
    bm = xc[:, W:W + N]
    cm = xc[:, W + N:W + 2 * N]

    rows = c * T + lax.broadcasted_iota(jnp.int32, (T, 128), 0)
    dtv = _softplus(dt_ref[...] + dtb_ref[...])
    dtv = jnp.where(rows >= FRONT_PAD, dtv, 0.0)
    a = -jnp.exp(alog_ref[...])
    da = dtv * a
    r_i = lax.broadcasted_iota(jnp.int32, (T, T), 0)
    c_i = lax.broadcasted_iota(jnp.int32, (T, T), 1)
    tril = r_i >= c_i
    acum = jnp.dot(tril.astype(F32), da, precision=lax.Precision.HIGHEST,
                   preferred_element_type=F32)
    acum_t = acum.T

    lane_t = lax.broadcasted_iota(jnp.int32, (T, 128), 1)
    lane_1 = lax.broadcasted_iota(jnp.int32, (1, 128), 1)

    def expand(v, lane):
        parts = [jnp.where(lane < SSD_HEADDIM, v[:, 2 * k:2 * k + 1], v[:, 2 * k + 1:2 * k + 2])
                 for k in range(E // 2)]
        return parts[0] if len(parts) == 1 else jnp.concatenate(parts, axis=1)

    xdt = xs * expand(dtv, lane_t)
    cb = lax.dot_general(cm.astype(BF16), bm.astype(BF16), _NT, preferred_element_type=F32)
    yd = []
    for e in range(E):
        seg = acum[:, e:e + 1] - acum_t[e:e + 1, :]
        g = (cb * jnp.exp(jnp.where(tril, seg, NEG))).astype(BF16)
        yd.append(jnp.dot(g, xdt[:, e * SSD_HEADDIM:(e + 1) * SSD_HEADDIM].astype(BF16),
                          preferred_element_type=F32))
    y_diag = jnp.concatenate(yd, axis=1)

    a_last = acum[T - 1:T, :]
    h_t = state_ref[...]
    y_off = jnp.dot(cm.astype(BF16), h_t.astype(BF16), preferred_element_type=F32)
    y_off = y_off * expand(jnp.exp(acum), lane_t)
    xw = (xdt * expand(jnp.exp(a_last - acum), lane_t)).astype(BF16)
    state_ref[...] = h_t * expand(jnp.exp(a_last), lane_1) + jnp.dot(
        bm.T.astype(BF16), xw, preferred_element_type=F32)

    y = y_diag + y_off + dskip_ref[...] * xs
    y = y * _silu(z_ref[...])
    ms = jnp.mean(y * y, axis=-1, keepdims=True)
    o_ref[...] = (y * lax.rsqrt(ms + NORM_EPS) * ng_ref[...]).astype(o_ref.dtype)


def ssd_mixer(proj, dtg, conv_w, conv_b, dt_bias, a_log, d_skip, norm_g, d_model):
    lp = proj.shape[0]
    d = d_model
    heads = d // SSD_HEADDIM
    e = heads // SSD_GROUPS
    w = d // SSD_GROUPS
    n = SSD_STATE
    t = SSD_ROWS
    xb, bb, cb = d // w, 2 * d // n, (2 * d + SSD_GROUPS * n) // n

    def padg(v):
        return jnp.pad(v.astype(F32).reshape(SSD_GROUPS, e), ((0, 0), (0, 128 - e))).reshape(1, -1)

    cw = conv_w.astype(F32)
    cbias = conv_b.astype(F32).reshape(1, -1)
    dsk = jnp.repeat(d_skip.astype(F32), SSD_HEADDIM).reshape(1, d)
    return pl.pallas_call(
        functools.partial(_ssd_kernel, T=t, E=e),
        grid=(SSD_GROUPS, lp // t),
        in_specs=[
            pl.BlockSpec((t, w), lambda g, c: (c, g)),
            pl.BlockSpec((t, w), lambda g, c: (c, xb + g)),
            pl.BlockSpec((t, n), lambda g, c: (c, bb + g)),
            pl.BlockSpec((t, n), lambda g, c: (c, cb + g)),
            pl.BlockSpec((t, 128), lambda g, c: (c, g)),
            pl.BlockSpec((SSD_CONV, w), lambda g, c: (0, g)),
            pl.BlockSpec((SSD_CONV, n), lambda g, c: (0, d // n + g)),
            pl.BlockSpec((SSD_CONV, n), lambda g, c: (0, (d + SSD_GROUPS * n) // n + g)),
            pl.BlockSpec((1, w), lambda g, c: (0, g)),
            pl.BlockSpec((1, n), lambda g, c: (0, d // n + g)),
            pl.BlockSpec((1, n), lambda g, c: (0, (d + SSD_GROUPS * n) // n + g)),
            pl.BlockSpec((1, 128), lambda g, c: (0, g)),
            pl.BlockSpec((1, 128), lambda g, c: (0, g)),
            pl.BlockSpec((1, w), lambda g, c: (0, g)),
            pl.BlockSpec((1, w), lambda g, c: (0, g)),
        ],
        out_specs=pl.BlockSpec((t, w), lambda g, c: (c, g)),
        out_shape=jax.ShapeDtypeStruct((lp, d), BF16),
        scratch_shapes=[pltpu.VMEM((t + 8, w + 2 * n), F32), pltpu.VMEM((n, w), F32)],
        compiler_params=_cparams("parallel", "arbitrary"),
        name="ssd_mixer",
    )(proj, proj, proj, proj, dtg, cw, cw, cw, cbias, cbias, cbias, padg(dt_bias), padg(a_log),
      dsk, norm_g.astype(F32).reshape(1, d))


def _s5_kernel(u_ref, m_ref, bs_ref, cs_ref, a1_ref, a2_ref, y_ref, ins_ref, int_ref, sp_ref, st_ref,
               *, GB, NBT):
    j = pl.program_id(1)

    @pl.when(j == 0)
    def _():
        st_ref[...] = jnp.zeros_like(st_ref)

    for k in range(GB):
        inp = jnp.dot(u_ref[k], bs_ref[k], preferred_element_type=F32)
        ins_ref[k * NBT:(k + 1) * NBT, :] = inp[:, 0:128]
        int_ref[k * NBT:(k + 1) * NBT, :] = inp[:, 128:256]

    a1 = a1_ref[...]
    a2 = a2_ref[...]

    def body(b, carry):
        s, t = carry
        rows = pl.ds(b, GB, stride=NBT)
        sp_ref[rows, :] = s
        return (a1 * s + a2 * t + ins_ref[rows, :], a1 * t - a2 * s + int_ref[rows, :])

    s, t = lax.fori_loop(0, NBT, body, (st_ref[0], st_ref[1]))
    st_ref[0] = s
    st_ref[1] = t

    for k in range(GB):
        y_ref[k] = (jnp.dot(u_ref[k], m_ref[k], preferred_element_type=F32)
                    + jnp.dot(sp_ref[k * NBT:(k + 1) * NBT, :].astype(BF16), cs_ref[k],
                              preferred_element_type=F32))


def _s5_operators(a_re, a_im, log_dt, b_re, b_im, c_re, c_im):
    hi = lax.Precision.HIGHEST
    g, p = a_re.shape
    kb = S5_BLOCK
    a_re, a_im = a_re.astype(F32), a_im.astype(F32)
    b_re, b_im = b_re.astype(F32), b_im.astype(F32)
    c_re, c_im = c_re.astype(F32), c_im.astype(F32)
    dt = jnp.exp(log_dt.astype(F32))[:, None]
    mag = jnp.exp(dt * a_re)
    ab_re, ab_im = mag * jnp.cos(dt * a_im), mag * jnp.sin(dt * a_im)
    den = a_re * a_re + a_im * a_im
    k_re = ((ab_re - 1.0) * a_re + ab_im * a_im) / den
    k_im = (ab_im * a_re - (ab_re - 1.0) * a_im) / den
    bb_re = k_re[..., None] * b_re - k_im[..., None] * b_im
    bb_im = k_re[..., None] * b_im + k_im[..., None] * b_re
    ks = jnp.arange(kb + 1, dtype=F32)[:, None, None]
    pm = jnp.exp(ks * (dt * a_re))
    pr, pi = pm * jnp.cos(ks * (dt * a_im)), pm * jnp.sin(ks * (dt * a_im))
    abr = pr[..., None] * bb_re - pi[..., None] * bb_im
    abi = pr[..., None] * bb_im + pi[..., None] * bb_re
    mk = (jnp.einsum('gjp,kgpi->kgji', c_re, abr[:kb], precision=hi)
          - jnp.einsum('gjp,kgpi->kgji', c_im, abi[:kb], precision=hi))
    tt = jnp.arange(kb)
    lag = tt[None, :] - tt[:, None]
    mfull = jnp.where((lag >= 0)[:, :, None, None, None], mk[jnp.clip(lag, 0)], 0.0)
    m = mfull.transpose(2, 0, 4, 1, 3).reshape(g, kb * S5_GROUP, kb * S5_GROUP)
    bs_re = abr[:kb][::-1].transpose(1, 0, 3, 2).reshape(g, kb * S5_GROUP, p)
    bs_im = abi[:kb][::-1].transpose(1, 0, 3, 2).reshape(g, kb * S5_GROUP, p)
    bs = jnp.concatenate([bs_re, bs_im, bs_im, bs_re], axis=-1)
    car = c_re[None] * pr[1:, :, None, :] - c_im[None] * pi[1:, :, None, :]
    cai = c_re[None] * pi[1:, :, None, :] + c_im[None] * pr[1:, :, None, :]
    cs = jnp.concatenate([car.transpose(1, 3, 0, 2), -cai.transpose(1, 3, 0, 2)], axis=1)
    cs = cs.reshape(g, 2 * p, kb * S5_GROUP)
    a1 = jnp.concatenate([pr[kb], pr[kb]], axis=-1)
    a2 = jnp.concatenate([-pi[kb], pi[kb]], axis=-1)
    return m.astype(BF16), bs.astype(BF16), cs.astype(BF16), a1, a2


def s5_ssm(u_blk, ops):
    m, bs, cs, a1, a2 = ops
    g, nb, kw = u_blk.shape
    gb = S5_GB
    nbt = nb // 2
    return pl.pallas_call(
        functools.partial(_s5_kernel, GB=gb, NBT=nbt),
        grid=(g // gb, nb // nbt),
        in_specs=[
            pl.BlockSpec((gb, nbt, kw), lambda i, j: (i, j, 0)),
            pl.BlockSpec((gb, kw, kw), lambda i, j: (i, 0, 0)),
            pl.BlockSpec((gb, kw, kw), lambda i, j: (i, 0, 0)),
            pl.BlockSpec((gb, 2 * S5_STATE, kw), lambda i, j: (i, 0, 0)),
            pl.BlockSpec((gb, 2 * S5_STATE), lambda i, j: (i, 0)),
            pl.BlockSpec((gb, 2 * S5_STATE), lambda i, j: (i, 0)),
        ],
        out_specs=pl.BlockSpec((gb, nbt, kw), lambda i, j: (i, j, 0)),
        out_shape=jax.ShapeDtypeStruct((g, nb, kw), F32),
        scratch_shapes=[pltpu.VMEM((gb * nbt, 2 * S5_STATE), F32), pltpu.VMEM((gb * nbt, 2 * S5_STATE), F32),
                        pltpu.VMEM((gb * nbt, 2 * S5_STATE), F32), pltpu.VMEM((2, gb, 2 * S5_STATE), F32)],
        compiler_params=_cparams("parallel", "arbitrary"),
        name="s5_ssm",
    )(u_blk, m, bs, cs, a1, a2)


def _glu_kernel(y_ref, u_ref, gate_ref, d_ref, w_ref, o_ref):
    y = y_ref[...] + d_ref[...] * u_ref[...]
    c0 = math.sqrt(2.0 / math.pi)
    g = 0.5 * y * (1.0 + jnp.tanh(c0 * (y + 0.044715 * (y * y * y))))
    lin = jnp.dot(g.astype(BF16), w_ref[...], preferred_element_type=F32)
    o_ref[...] = (g * _sigmoid(lin) * _silu(gate_ref[...])).astype(o_ref.dtype)


def s5_glu(y_ssm, proj, d_skip, w_glu, u_off, gate_off):
    lp, sw = y_ssm.shape
    tm = _pick(lp, (MM_ROWS,))
    return pl.pallas_call(
        _glu_kernel,
        grid=(lp // tm,),
        in_specs=[
            pl.BlockSpec((tm, sw), lambda i: (i, 0)),
            pl.BlockSpec((tm, sw), lambda i: (i, u_off // sw)),
            pl.BlockSpec((tm, sw), lambda i: (i, gate_off // sw)),
            pl.BlockSpec((1, sw), lambda i: (0, 0)),
            pl.BlockSpec((sw, sw), lambda i: (0, 0)),
        ],
        out_specs=pl.BlockSpec((tm, sw), lambda i: (i, 0)),
        out_shape=jax.ShapeDtypeStruct((lp, sw), BF16),
        compiler_params=_cparams("parallel"),
        name="s5_glu",
    )(y_ssm, proj, proj, d_skip.astype(F32).reshape(1, sw), w_glu.astype(BF16))


def _rope(r, cos, sin):
    half = MLA_ROPE // 2
    x1, x2 = r[:, :half], r[:, half:]
    return jnp.concatenate([x1 * cos - x2 * sin, x2 * cos + x1 * sin], axis=1)


def _q_kernel(cq_ref, g_ref, w_ref, cos_ref, sin_ref, o_ref, xn_ref, *, scale):
    @pl.when(pl.program_id(1) == 0)
    def _():
        x = cq_ref[...]
        ms = jnp.mean(x * x, axis=-1, keepdims=True)
        xn_ref[...] = (x * lax.rsqrt(ms + NORM_EPS) * g_ref[...]).astype(BF16)

    acc = lax.dot_general(w_ref[0], xn_ref[...], _NT, preferred_element_type=F32)
    tm = acc.shape[1]
    half = MLA_ROPE // 2
    cos, sin = cos_ref[...], sin_ref[...]
    tail_row = lax.broadcasted_iota(jnp.int32, (MLA_QK_PAD - MLA_NOPE - MLA_ROPE, tm), 0)
    tail = jnp.where(tail_row == 0, 1.0, 0.0)
    for hh in range(2):
        qn = acc[hh * MLA_NOPE:(hh + 1) * MLA_NOPE, :]
        r0 = 2 * MLA_NOPE + hh * MLA_ROPE
        x1, x2 = acc[r0:r0 + half, :], acc[r0 + half:r0 + MLA_ROPE, :]
        qr = jnp.concatenate([x1 * cos - x2 * sin, x2 * cos + x1 * sin], axis=0)
        o_ref[hh] = jnp.concatenate([qn * scale, qr * scale, tail], axis=0).astype(o_ref.dtype)


def _kv_kernel(ckv_ref, g_ref, wk_ref, wvt_ref, kr_ref, cos_ref, sin_ref, k_ref, vt_ref, xn_ref):
    @pl.when(pl.program_id(1) == 0)
    def _():
        x = ckv_ref[...]
        ms = jnp.mean(x * x, axis=-1, keepdims=True)
        xn_ref[...] = (x * lax.rsqrt(ms + NORM_EPS) * g_ref[...]).astype(BF16)

    xn = xn_ref[...]
    k = jnp.dot(xn, wk_ref[0], preferred_element_type=F32)
    tm = k.shape[0]
    kr = _rope(kr_ref[...][:, 0:MLA_ROPE], cos_ref[...], sin_ref[...])
    tail_shape = (tm, MLA_QK_PAD - MLA_NOPE - MLA_ROPE)
    rows = pl.program_id(0) * tm + lax.broadcasted_iota(jnp.int32, tail_shape, 0)
    lane = lax.broadcasted_iota(jnp.int32, tail_shape, 1)
    pad = jnp.where((lane == 0) & (rows < FRONT_PAD), NEG, 0.0)
    vt = lax.dot_general(wvt_ref[0], xn, _NT, preferred_element_type=F32)
    for hh in range(2):
        kn = k[:, hh * MLA_NOPE:(hh + 1) * MLA_NOPE]
        k_ref[hh, 0] = jnp.concatenate([kn, kr, pad], axis=1).astype(k_ref.dtype)
        vt_ref[hh, 0] = vt[hh * MLA_V:(hh + 1) * MLA_V, :].astype(vt_ref.dtype)


def _attn_kernel(q_ref, k_ref, vt_ref, gate_ref, o_ref, s_ref, m_ref, l_ref, acc_ref, *, TK, NQ):
    iq = pl.program_id(1)
    k_row = lax.broadcasted_iota(jnp.int32, (TK, TK), 0)
    q_col = lax.broadcasted_iota(jnp.int32, (TK, TK), 1)
    diag_ok = (k_row // CHUNK) <= (q_col // CHUNK)

    def scores(kt, c):
        return jnp.dot(kt, q_ref[0, :, c * TK:(c + 1) * TK], preferred_element_type=F32)

    def update(c, s, vt):
        m = m_ref[c]
        m_new = jnp.maximum(m, jnp.max(s, axis=0, keepdims=True))
        alpha = jnp.exp2(m - m_new)
        p = jnp.exp2(s - m_new)
        m_ref[c] = m_new
        l_ref[c] = alpha * l_ref[c] + jnp.sum(p, axis=0, keepdims=True)
        acc_ref[c] = alpha * acc_ref[c] + jnp.dot(vt, p.astype(BF16), preferred_element_type=F32)

    m_ref[...] = jnp.full(m_ref.shape, NEG, F32)
    l_ref[...] = jnp.zeros(l_ref.shape, F32)
    acc_ref[...] = jnp.zeros(acc_ref.shape, F32)
    n_full = NQ * iq
    for c in range(NQ):
        s_ref[c] = scores(k_ref[0, 0], c)

    def full_body(j, _):
        kt_next = k_ref[0, j + 1]
        s_next = [scores(kt_next, c) for c in range(NQ)]
        vt = vt_ref[0, j]
        for c in range(NQ):
            update(c, s_ref[c], vt)
        for c in range(NQ):
            s_ref[c] = s_next[c]
        return 0

    lax.fori_loop(0, n_full, full_body, 0)
    for d in range(NQ):
        vt = vt_ref[0, n_full + d]
        if d == 0:
            s_d = [s_ref[c] for c in range(NQ)]
        else:
            kt = k_ref[0, n_full + d]
            s_d = [None] * d + [scores(kt, c) for c in range(d, NQ)]
        for c in range(d, NQ):
            update(c, jnp.where(diag_ok, s_d[c], NEG) if c == d else s_d[c], vt)
    for c in range(NQ):
        o = (acc_ref[c] * (1.0 / l_ref[c])).T
        rows = slice(c * TK, (c + 1) * TK)
        o_ref[rows, :] = (o * _silu(gate_ref[rows, :])).astype(o_ref.dtype)


def mla_mixer(proj, kr_raw, q_norm, w_uq, kv_norm, w_ukv, d_model, row_pos):
    lp = proj.shape[0]
    d = d_model
    h = d // MLA_V
    qr, kvr = d // 4, d // 8
    ta = ATT_ROWS
    nt = lp // ta
    half = MLA_ROPE // 2
    inv_freq = ROPE_BASE ** (-jnp.arange(0, MLA_ROPE, 2, dtype=F32) / MLA_ROPE)
    ang = row_pos[:, None] * inv_freq[None, :]
    cos, sin = jnp.cos(ang), jnp.sin(ang)

    wq = w_uq.reshape(qr, h // 2, 2, MLA_NOPE + MLA_ROPE)
    wq = jnp.concatenate([wq[..., :MLA_NOPE].reshape(qr, h // 2, 2 * MLA_NOPE),
                          wq[..., MLA_NOPE:].reshape(qr, h // 2, 2 * MLA_ROPE)], axis=-1)
    wqt = wq.transpose(1, 2, 0).astype(BF16)
    wkv = w_ukv.reshape(kvr, h // 2, 2, MLA_NOPE + MLA_V)
    wk = wkv[..., :MLA_NOPE].reshape(kvr, h // 2, 2 * MLA_NOPE).transpose(1, 0, 2).astype(BF16)
    wvt = wkv[..., MLA_NOPE:].reshape(kvr, h // 2, 2 * MLA_V).transpose(1, 2, 0).astype(BF16)

    tmq = _pick(lp, (MM_ROWS,))
    scale = (MLA_NOPE + MLA_ROPE) ** -0.5 * math.log2(math.e)
    qt3 = pl.pallas_call(
        functools.partial(_q_kernel, scale=scale),
        grid=(lp // tmq, h // 2),
        in_specs=[
            pl.BlockSpec((tmq, qr), lambda i, j: (i, 0)),
            pl.BlockSpec((1, qr), lambda i, j: (0, 0)),
            pl.BlockSpec((1, 2 * (MLA_NOPE + MLA_ROPE), qr), lambda i, j: (j, 0, 0)),
            pl.BlockSpec((half, tmq), lambda i, j: (0, i)),
            pl.BlockSpec((half, tmq), lambda i, j: (0, i)),
        ],
        out_specs=pl.BlockSpec((2, MLA_QK_PAD, tmq), lambda i, j: (j, 0, i)),
        out_shape=jax.ShapeDtypeStruct((h, MLA_QK_PAD, lp), BF16),
        scratch_shapes=[pltpu.VMEM((tmq, qr), BF16)],
        compiler_params=_cparams("parallel", "arbitrary"),
        name="mla_q",
    )(proj, q_norm.astype(F32).reshape(1, qr), wqt, cos.T, sin.T)

    k4, vt4 = pl.pallas_call(
        _kv_kernel,
        grid=(nt, h // 2),
        in_specs=[
            pl.BlockSpec((ta, kvr), lambda i, j: (i, qr // kvr)),
            pl.BlockSpec((1, kvr), lambda i, j: (0, 0)),
            pl.BlockSpec((1, kvr, 2 * MLA_NOPE), lambda i, j: (j, 0, 0)),
            pl.BlockSpec((1, 2 * MLA_V, kvr), lambda i, j: (j, 0, 0)),
            pl.BlockSpec((ta, 128), lambda i, j: (i, 0)),
            pl.BlockSpec((ta, half), lambda i, j: (i, 0)),
            pl.BlockSpec((ta, half), lambda i, j: (i, 0)),
        ],
        out_specs=[pl.BlockSpec((2, 1, ta, MLA_QK_PAD), lambda i, j: (j, i, 0, 0)),
                   pl.BlockSpec((2, 1, MLA_V, ta), lambda i, j: (j, i, 0, 0))],
        out_shape=[jax.ShapeDtypeStruct((h, nt, ta, MLA_QK_PAD), BF16),
                   jax.ShapeDtypeStruct((h, nt, MLA_V, ta), BF16)],
        scratch_shapes=[pltpu.VMEM((ta, kvr), BF16)],
        compiler_params=_cparams("parallel", "arbitrary"),
        name="mla_kv",
    )(proj, kv_norm.astype(F32).reshape(1, kvr), wk, wvt, kr_raw, cos, sin)

    gate_blk = (qr + kvr) // MLA_V
    nq = ATT_Q_TILES
    tq = nq * ta
    return pl.pallas_call(
        functools.partial(_attn_kernel, TK=ta, NQ=nq),
        grid=(h, lp // tq),
        in_specs=[
            pl.BlockSpec((1, MLA_QK_PAD, tq), lambda hh, i: (hh, 0, i)),
            pl.BlockSpec((1, nt, ta, MLA_QK_PAD), lambda hh, i: (hh, 0, 0, 0)),
            pl.BlockSpec((1, nt, MLA_V, ta), lambda hh, i: (hh, 0, 0, 0)),
            pl.BlockSpec((tq, MLA_V), lambda hh, i: (i, gate_blk + hh)),
        ],
        out_specs=pl.BlockSpec((tq, MLA_V), lambda hh, i: (i, hh)),
        out_shape=jax.ShapeDtypeStruct((lp, d), BF16),
        scratch_shapes=[pltpu.VMEM((nq, ta, ta), F32), pltpu.VMEM((nq, 1, ta), F32),
                        pltpu.VMEM((nq, 1, ta), F32), pltpu.VMEM((nq, MLA_V, ta), F32)],
        compiler_params=_cparams("parallel", "parallel"),
        name="mla_attention",
    )(qt3, k4, vt4, proj)


def _hybrid_layer(h, norm_g, w_in, conv_w, conv_b, dt_bias, a_log, ssd_d, ssd_norm, s5_ops, s5_d,
                  w_glu, w_out):
    lp, d = h.shape
    heads = d // SSD_HEADDIM
    e = heads // SSD_GROUPS
    conv_dim = d + 2 * SSD_GROUPS * SSD_STATE
    s5w = d // 2
    o_dt = d + conv_dim
    hn = rms_norm_rows(h, norm_g, BF16, NORM_ROWS)
    w_main = jnp.concatenate([w_in[:, :o_dt], w_in[:, o_dt + heads:]], axis=1).astype(BF16)
    w_dt = w_in[:, o_dt:o_dt + heads].reshape(d, SSD_GROUPS, e)
    w_dt = jnp.pad(w_dt, ((0, 0), (0, 0), (0, 128 // SSD_GROUPS - e))).reshape(d, 128).astype(BF16)
    proj = matmul([hn], [w_main], F32, name="hyb_in_proj")
    dt_raw = matmul([hn], [w_dt], F32, name="hyb_dt_proj")
    dtg = jnp.pad(dt_raw.reshape(lp, SSD_GROUPS, 128 // SSD_GROUPS),
                  ((0, 0), (0, 0), (0, 128 - 128 // SSD_GROUPS))).reshape(lp, SSD_GROUPS * 128)
    y_a = ssd_mixer(proj, dtg, conv_w, conv_b, dt_bias, a_log, ssd_d, ssd_norm, d)

    u_off = o_dt
    gate_off = o_dt + s5w
    g5 = s5w // S5_GROUP
    nb = lp // S5_BLOCK
    u_blk = proj[:, u_off:u_off + s5w].astype(BF16).reshape(nb, S5_BLOCK, g5, S5_GROUP)
    u_blk = u_blk.transpose(2, 0, 1, 3).reshape(g5, nb, S5_BLOCK * S5_GROUP)
    y_blk = s5_ssm(u_blk, s5_ops)
    y_ssm = y_blk.reshape(g5, nb, S5_BLOCK, S5_GROUP).transpose(1, 2, 0, 3).reshape(lp, s5w)
    y_b = s5_glu(y_ssm, proj, s5_d, w_glu, u_off, gate_off)
    w_o = w_out.astype(BF16)
    return matmul([y_a, y_b], [w_o[:d], w_o[d:]], F32, res=h, name="hyb_out_proj")


def _mla_layer(h, norm_g, w_in, q_norm, w_uq, kv_norm, w_ukv, w_out, row_pos):
    lp, d = h.shape
    qr, kvr = d // 4, d // 8
    hn = rms_norm_rows(h, norm_g, BF16, NORM_ROWS)
    o_kr = qr + kvr
    w_main = jnp.concatenate([w_in[:, :o_kr], w_in[:, o_kr + MLA_ROPE:]], axis=1).astype(BF16)
    w_kr = jnp.pad(w_in[:, o_kr:o_kr + MLA_ROPE], ((0, 0), (0, 128 - MLA_ROPE))).astype(BF16)
    proj = matmul([hn], [w_main], F32, name="mla_in_proj")
    kr_raw = matmul([hn], [w_kr], F32, name="mla_kr_proj")
    y_c = mla_mixer(proj, kr_raw, q_norm, w_uq, kv_norm, w_ukv, d, row_pos)
    return matmul([y_c], [w_out.astype(BF16)], F32, res=h, name="mla_out_proj")


def kernel(x, meta, hyb_norm, hyb_w_in, ssd_conv_w, ssd_conv_b, ssd_dt_bias, ssd_a_log, ssd_d, ssd_norm, s5_a_re, s5_a_im, s5_log_dt, s5_b_re, s5_b_im, s5_c_re, s5_c_im, s5_d, s5_w_glu, hyb_w_out, mla_norm, mla_w_in, mla_q_norm, mla_w_uq, mla_kv_norm, mla_w_ukv, mla_w_out, final_norm):
    bsz, seq, d = x.shape
    assert bsz == 1 and seq % CHUNK == 0 and meta.shape == (N_META, d)
    depth = hyb_norm.shape[0] + mla_norm.shape[0]
    first = FRONT_PAD + N_META
    lp = -(-(first + seq) // ROW_ALIGN) * ROW_ALIGN
    h = jnp.concatenate([jnp.zeros((FRONT_PAD, d), F32), meta.astype(F32), x[0].astype(F32),
                         jnp.zeros((lp - first - seq, d), F32)], axis=0)
    row_pos = jnp.arange(lp, dtype=jnp.int32).astype(F32) - float(FRONT_PAD)
    for layer in range(depth):
        i = layer // 2
        if layer % 2 == 0:
            ops = _s5_operators(s5_a_re[i], s5_a_im[i], s5_log_dt[i], s5_b_re[i], s5_b_im[i],
                                s5_c_re[i], s5_c_im[i])
            h = _hybrid_layer(h, hyb_norm[i], hyb_w_in[i], ssd_conv_w[i], ssd_conv_b[i],
                              ssd_dt_bias[i], ssd_a_log[i], ssd_d[i], ssd_norm[i], ops, s5_d[i],
                              s5_w_glu[i], hyb_w_out[i])
        else:
            h = _mla_layer(h, mla_norm[i], mla_w_in[i], mla_q_norm[i], mla_w_uq[i], mla_kv_norm[i],
                           mla_w_ukv[i], mla_w_out[i], row_pos)
    out = rms_norm_rows(h, final_norm, x.dtype, CHUNK, row_block_offset=first // CHUNK, n_rows=seq)
    return out.reshape(1, seq, d)
```

```python
import functools
import math

import jax
import jax.numpy as jnp
from jax import lax
from jax.experimental import pallas as pl
from jax.experimental.pallas import tpu as pltpu

F32 = jnp.float32
BF16 = jnp.bfloat16

CHUNK = 64
N_META = 16
FRONT_PAD = (-N_META) % CHUNK
NORM_EPS = 1e-6

SSD_HEADDIM = 64
SSD_GROUPS = 8
SSD_STATE = 128
SSD_CONV = 4

S5_GROUP = 16
S5_STATE = 64
S5_BLOCK = 16
S5_GB = 8

MLA_NOPE = 128
MLA_ROPE = 64
MLA_V = 128
MLA_QK_PAD = 256
MLA_VT_ROWS = MLA_V + 16
ROPE_BASE = 10000.0

ROW_ALIGN = 1280
MM_ROWS = 640
NORM_ROWS = 256
SSD_ROWS = 128
ATT_ROWS = 256
ATT_Q_TILES = 5
ATT_UNROLL = 4
MLA_HPS = 4
NEG = -1e30
VMEM_LIMIT = 56 * 1024 * 1024

_NT = (((1,), (1,)), ((), ()))


def _cparams(*sem):
    return pltpu.CompilerParams(dimension_semantics=sem, vmem_limit_bytes=VMEM_LIMIT)


def _silu(x):
    return x * (1.0 / (1.0 + jnp.exp(-x)))


def _sigmoid(x):
    return 1.0 / (1.0 + jnp.exp(-x))


def _softplus(x):
    return jnp.maximum(x, 0.0) + jnp.log(1.0 + jnp.exp(-jnp.abs(x)))


def _pick(n, cands):
    for c in cands:
        if n % c == 0:
            return c
    raise ValueError(f"no tile for {n} in {cands}")


def _rms_kernel(x_ref, g_ref, o_ref):
    x = x_ref[...]
    ms = jnp.mean(x * x, axis=-1, keepdims=True)
    o_ref[...] = (x * lax.rsqrt(ms + NORM_EPS) * g_ref[...]).astype(o_ref.dtype)


def rms_norm_rows(x, g, out_dtype, rows, row_block_offset=0, n_rows=None):
    n, d = x.shape
    n_rows = n if n_rows is None else n_rows
    return pl.pallas_call(
        _rms_kernel,
        grid=(n_rows // rows,),
        in_specs=[pl.BlockSpec((rows, d), lambda i: (i + row_block_offset, 0)),
                  pl.BlockSpec((1, d), lambda i: (0, 0))],
        out_specs=pl.BlockSpec((rows, d), lambda i: (i, 0)),
        out_shape=jax.ShapeDtypeStruct((n_rows, d), out_dtype),
        compiler_params=_cparams("parallel"),
        name="rms_norm",
    )(x, g.reshape(1, d).astype(F32))


def _mm_kernel(*refs, nx, has_res):
    o_ref = refs[-1]
    acc = None
    for xr, wr in zip(refs[:nx], refs[nx:2 * nx]):
        d = jnp.dot(xr[...], wr[...], preferred_element_type=F32)
        acc = d if acc is None else acc + d
    if has_res:
        acc = acc + refs[2 * nx][...]
    o_ref[...] = acc.astype(o_ref.dtype)


def matmul(xs, ws, out_dtype, res=None, name="matmul"):
    m = xs[0].shape[0]
    n = ws[0].shape[1]
    tm = _pick(m, (MM_ROWS,))
    tn = _pick(n, (512, 256, 128))
    in_specs = [pl.BlockSpec((tm, x.shape[1]), lambda i, j: (i, 0)) for x in xs]
    in_specs += [pl.BlockSpec((w.shape[0], tn), lambda i, j: (0, j)) for w in ws]
    args = list(xs) + list(ws)
    if res is not None:
        in_specs.append(pl.BlockSpec((tm, tn), lambda i, j: (i, j)))
        args.append(res)
    return pl.pallas_call(
        functools.partial(_mm_kernel, nx=len(xs), has_res=res is not None),
        grid=(m // tm, n // tn),
        in_specs=in_specs,
        out_specs=pl.BlockSpec((tm, tn), lambda i, j: (i, j)),
        out_shape=jax.ShapeDtypeStruct((m, n), out_dtype),
        compiler_params=_cparams("parallel", "parallel"),
        name=name,
    )(*args)


def _ssd_kernel(z_ref, x_ref, b_ref, c_ref, dt_ref, cwx_ref, cwb_ref, cwc_ref, cbx_ref, cbb_ref,
                cbc_ref, dtb_ref, alog_ref, dskip_ref, ng_ref, o_ref, ext_ref, state_ref, *, T, E):
    c = pl.program_id(1)
    W = E * SSD_HEADDIM
    N = SSD_STATE

    @pl.when(c == 0)
    def _():
        ext_ref[0:8, :] = jnp.zeros((8, W + 2 * N), F32)
        state_ref[...] = jnp.zeros_like(state_ref)

    @pl.when(c > 0)
    def _():
        ext_ref[0:8, :] = ext_ref[T:T + 8, :]

    ext_ref[8:T + 8, 0:W] = x_ref[...]
    ext_ref[8:T + 8, W:W + N] = b_ref[...]
    ext_ref[8:T + 8, W + N:W + 2 * N] = c_ref[...]

    cw = jnp.concatenate([cwx_ref[...], cwb_ref[...], cwc_ref[...]], axis=1)
    acc = jnp.concatenate([cbx_ref[...], cbb_ref[...], cbc_ref[...]], axis=1)
    for k in range(SSD_CONV):
        acc = acc + cw[k:k + 1, :] * ext_ref[pl.ds(8 - (SSD_CONV - 1) + k, T), :]
    xc = _silu(acc)
    xs = xc[:, 0:W]
    bm = xc[:, W:W + N]
    cm = xc[:, W + N:W + 2 * N]

    rows = c * T + lax.broadcasted_iota(jnp.int32, (T, 128), 0)
    dtv = _softplus(dt_ref[...] + dtb_ref[...])
    dtv = jnp.where(rows >= FRONT_PAD, dtv, 0.0)
    a = -jnp.exp(alog_ref[...])
    da = dtv * a
    r_i = lax.broadcasted_iota(jnp.int32, (T, T), 0)
    c_i = lax.broadcasted_iota(jnp.int32, (T, T), 1)
    tril = r_i >= c_i
    acum = jnp.dot(tril.astype(F32), da, precision=lax.Precision.HIGHEST,
                   preferred_element_type=F32)
    acum_t = acum.T

    lane_t = lax.broadcasted_iota(jnp.int32, (T, 128), 1)
    lane_1 = lax.broadcasted_iota(jnp.int32, (1, 128), 1)

    def expand(v, lane):
        parts = [jnp.where(lane < SSD_HEADDIM, v[:, 2 * k:2 * k + 1], v[:, 2 * k + 1:2 * k + 2])
                 for k in range(E // 2)]
        return parts[0] if len(parts) == 1 else jnp.concatenate(parts, axis=1)

    xdt = xs * expand(dtv, lane_t)
    cb = lax.dot_general(cm.astype(BF16), bm.astype(BF16), _NT, preferred_element_type=F32)
    yd = []
    for e in range(E):
        seg = acum[:, e:e + 1] - acum_t[e:e + 1, :]
        g = (cb * jnp.exp(jnp.where(tril, seg, NEG))).astype(BF16)
        yd.append(jnp.dot(g, xdt[:, e * SSD_HEADDIM:(e + 1) * SSD_HEADDIM].astype(BF16),
                          preferred_element_type=F32))
    y_diag = jnp.concatenate(yd, axis=1)

    a_last = acum[T - 1:T, :]
    h_t = state_ref[...]
    y_off = jnp.dot(cm.astype(BF16), h_t.astype(BF16), preferred_element_type=F32)
    y_off = y_off * expand(jnp.exp(acum), lane_t)
    xw = (xdt * expand(jnp.exp(a_last - acum), lane_t)).astype(BF16)
    state_ref[...] = h_t * expand(jnp.exp(a_last), lane_1) + jnp.dot(
        bm.T.astype(BF16), xw, preferred_element_type=F32)

    y = y_diag + y_off + dskip_ref[...] * xs
    y = y * _silu(z_ref[...])
    ms = jnp.mean(y * y, axis=-1, keepdims=True)
    o_ref[...] = (y * lax.rsqrt(ms + NORM_EPS) * ng_ref[...]).astype(o_ref.dtype)


def ssd_mixer(proj, dtg, conv_w, conv_b, dt_bias, a_log, d_skip, norm_g, d_model):
    lp = proj.shape[0]
    d = d_model
    heads = d // SSD_HEADDIM
    e = heads // SSD_GROUPS
    w = d // SSD_GROUPS
    n = SSD_STATE
    t = SSD_ROWS
    xb, bb, cb = d // w, 2 * d // n, (2 * d + SSD_GROUPS * n) // n

    def padg(v):
        return jnp.pad(v.astype(F32).reshape(SSD_GROUPS, e), ((0, 0), (0, 128 - e))).reshape(1, -1)

    cw = conv_w.astype(F32)
    cbias = conv_b.astype(F32).reshape(1, -1)
    dsk = jnp.repeat(d_skip.astype(F32), SSD_HEADDIM).reshape(1, d)
    return pl.pallas_call(
        functools.partial(_ssd_kernel, T=t, E=e),
        grid=(SSD_GROUPS, lp // t),
        in_specs=[
            pl.BlockSpec((t, w), lambda g, c: (c, g)),
            pl.BlockSpec((t, w), lambda g, c: (c, xb + g)),
            pl.BlockSpec((t, n), lambda g, c: (c, bb + g)),
            pl.BlockSpec((t, n), lambda g, c: (c, cb + g)),
            pl.BlockSpec((t, 128), lambda g, c: (c, g)),
            pl.BlockSpec((SSD_CONV, w), lambda g, c: (0, g)),
            pl.BlockSpec((SSD_CONV, n), lambda g, c: (0, d // n + g)),
            pl.BlockSpec((SSD_CONV, n), lambda g, c: (0, (d + SSD_GROUPS * n) // n + g)),
            pl.BlockSpec((1, w), lambda g, c: (0, g)),
            pl.BlockSpec((1, n), lambda g, c: (0, d // n + g)),
            pl.BlockSpec((1, n), lambda g, c: (0, (d + SSD_GROUPS * n) // n + g)),
            pl.BlockSpec((1, 128), lambda g, c: (0, g)),
            pl.BlockSpec((1, 128), lambda g, c: (0, g)),
            pl.BlockSpec((1, w), lambda g, c: (0, g)),
            pl.BlockSpec((1, w), lambda g, c: (0, g)),
        ],
        out_specs=pl.BlockSpec((t, w), lambda g, c: (c, g)),
        out_shape=jax.ShapeDtypeStruct((lp, d), BF16),
        scratch_shapes=[pltpu.VMEM((t + 8, w + 2 * n), F32), pltpu.VMEM((n, w), F32)],
        compiler_params=_cparams("parallel", "arbitrary"),
        name="ssd_mixer",
    )(proj, proj, proj, proj, dtg, cw, cw, cw, cbias, cbias, cbias, padg(dt_bias), padg(a_log),
      dsk, norm_g.astype(F32).reshape(1, d))


def _s5_kernel(u_ref, m_ref, bs_ref, cs_ref, a1_ref, a2_ref, y_ref, ins_ref, int_ref, sp_ref, st_ref,
               *, GB, NBT):
    j = pl.program_id(1)

    @pl.when(j == 0)
    def _():
        st_ref[...] = jnp.zeros_like(st_ref)

    for k in range(GB):
        inp = jnp.dot(u_ref[k], bs_ref[k], preferred_element_type=F32)
        ins_ref[k * NBT:(k + 1) * NBT, :] = inp[:, 0:128]
        int_ref[k * NBT:(k + 1) * NBT, :] = inp[:, 128:256]

    a1 = a1_ref[...]
    a2 = a2_ref[...]

    def body(b, carry):
        s, t = carry
        rows = pl.ds(b, GB, stride=NBT)
        sp_ref[rows, :] = s
        return (a1 * s + a2 * t + ins_ref[rows, :], a1 * t - a2 * s + int_ref[rows, :])

    s, t = lax.fori_loop(0, NBT, body, (st_ref[0], st_ref[1]))
    st_ref[0] = s
    st_ref[1] = t

    for k in range(GB):
        y_ref[k] = (jnp.dot(u_ref[k], m_ref[k], preferred_element_type=F32)
                    + jnp.dot(sp_ref[k * NBT:(k + 1) * NBT, :].astype(BF16), cs_ref[k],
                              preferred_element_type=F32))


def _s5_operators(a_re, a_im, log_dt, b_re, b_im, c_re, c_im):
    hi = lax.Precision.HIGHEST
    g, p = a_re.shape
    kb = S5_BLOCK
    a_re, a_im = a_re.astype(F32), a_im.astype(F32)
    b_re, b_im = b_re.astype(F32), b_im.astype(F32)
    c_re, c_im = c_re.astype(F32), c_im.astype(F32)
    dt = jnp.exp(log_dt.astype(F32))[:, None]
    mag = jnp.exp(dt * a_re)
    ab_re, ab_im = mag * jnp.cos(dt * a_im), mag * jnp.sin(dt * a_im)
    den = a_re * a_re + a_im * a_im
    k_re = ((ab_re - 1.0) * a_re + ab_im * a_im) / den
    k_im = (ab_im * a_re - (ab_re - 1.0) * a_im) / den
    bb_re = k_re[..., None] * b_re - k_im[..., None] * b_im
    bb_im = k_re[..., None] * b_im + k_im[..., None] * b_re
    ks = jnp.arange(kb + 1, dtype=F32)[:, None, None]
    pm = jnp.exp(ks * (dt * a_re))
    pr, pi = pm * jnp.cos(ks * (dt * a_im)), pm * jnp.sin(ks * (dt * a_im))
    abr = pr[..., None] * bb_re - pi[..., None] * bb_im
    abi = pr[..., None] * bb_im + pi[..., None] * bb_re
    mk = (jnp.einsum('gjp,kgpi->kgji', c_re, abr[:kb], precision=hi)
          - jnp.einsum('gjp,kgpi->kgji', c_im, abi[:kb], precision=hi))
    tt = jnp.arange(kb)
    lag = tt[None, :] - tt[:, None]
    mfull = jnp.where((lag >= 0)[:, :, None, None, None], mk[jnp.clip(lag, 0)], 0.0)
    m = mfull.transpose(2, 0, 4, 1, 3).reshape(g, kb * S5_GROUP, kb * S5_GROUP)
    bs_re = abr[:kb][::-1].transpose(1, 0, 3, 2).reshape(g, kb * S5_GROUP, p)
    bs_im = abi[:kb][::-1].transpose(1, 0, 3, 2).reshape(g, kb * S5_GROUP, p)
    bs = jnp.concatenate([bs_re, bs_im, bs_im, bs_re], axis=-1)
    car = c_re[None] * pr[1:, :, None, :] - c_im[None] * pi[1:, :, None, :]
    cai = c_re[None] * pi[1:, :, None, :] + c_im[None] * pr[1:, :, None, :]
    cs = jnp.concatenate([car.transpose(1, 3, 0, 2), -cai.transpose(1, 3, 0, 2)], axis=1)
    cs = cs.reshape(g, 2 * p, kb * S5_GROUP)
    a1 = jnp.concatenate([pr[kb], pr[kb]], axis=-1)
    a2 = jnp.concatenate([-pi[kb], pi[kb]], axis=-1)
    return m.astype(BF16), bs.astype(BF16), cs.astype(BF16), a1, a2


def s5_ssm(u_blk, ops):
    m, bs, cs, a1, a2 = ops
    g, nb, kw = u_blk.shape
    gb = S5_GB
    nbt = nb // 2
    return pl.pallas_call(
        functools.partial(_s5_kernel, GB=gb, NBT=nbt),
        grid=(g // gb, nb // nbt),
        in_specs=[
            pl.BlockSpec((gb, nbt, kw), lambda i, j: (i, j, 0)),
            pl.BlockSpec((gb, kw, kw), lambda i, j: (i, 0, 0)),
            pl.BlockSpec((gb, kw, kw), lambda i, j: (i, 0, 0)),
            pl.BlockSpec((gb, 2 * S5_STATE, kw), lambda i, j: (i, 0, 0)),
            pl.BlockSpec((gb, 2 * S5_STATE), lambda i, j: (i, 0)),
            pl.BlockSpec((gb, 2 * S5_STATE), lambda i, j: (i, 0)),
        ],
        out_specs=pl.BlockSpec((gb, nbt, kw), lambda i, j: (i, j, 0)),
        out_shape=jax.ShapeDtypeStruct((g, nb, kw), F32),
        scratch_shapes=[pltpu.VMEM((gb * nbt, 2 * S5_STATE), F32), pltpu.VMEM((gb * nbt, 2 * S5_STATE), F32),
                        pltpu.VMEM((gb * nbt, 2 * S5_STATE), F32), pltpu.VMEM((2, gb, 2 * S5_STATE), F32)],
        compiler_params=_cparams("parallel", "arbitrary"),
        name="s5_ssm",
    )(u_blk, m, bs, cs, a1, a2)


def _glu_kernel(y_ref, u_ref, gate_ref, d_ref, w_ref, o_ref):
    y = y_ref[...] + d_ref[...] * u_ref[...]
    c0 = math.sqrt(2.0 / math.pi)
    g = 0.5 * y * (1.0 + jnp.tanh(c0 * (y + 0.044715 * (y * y * y))))
    lin = jnp.dot(g.astype(BF16), w_ref[...], preferred_element_type=F32)
    o_ref[...] = (g * _sigmoid(lin) * _silu(gate_ref[...])).astype(o_ref.dtype)


def s5_glu(y_ssm, proj, d_skip, w_glu, u_off, gate_off):
    lp, sw = y_ssm.shape
    tm = _pick(lp, (MM_ROWS,))
    return pl.pallas_call(
        _glu_kernel,
        grid=(lp // tm,),
        in_specs=[
            pl.BlockSpec((tm, sw), lambda i: (i, 0)),
            pl.BlockSpec((tm, sw), lambda i: (i, u_off // sw)),
            pl.BlockSpec((tm, sw), lambda i: (i, gate_off // sw)),
            pl.BlockSpec((1, sw), lambda i: (0, 0)),
            pl.BlockSpec((sw, sw), lambda i: (0, 0)),
        ],
        out_specs=pl.BlockSpec((tm, sw), lambda i: (i, 0)),
        out_shape=jax.ShapeDtypeStruct((lp, sw), BF16),
        compiler_params=_cparams("parallel"),
        name="s5_glu",
    )(y_ssm, proj, proj, d_skip.astype(F32).reshape(1, sw), w_glu.astype(BF16))


def _rope(r, cos, sin):
    half = MLA_ROPE // 2
    x1, x2 = r[:, :half], r[:, half:]
    return jnp.concatenate([x1 * cos - x2 * sin, x2 * cos + x1 * sin], axis=1)


def _q_kernel(cq_ref, g_ref, w_ref, cos_ref, sin_ref, o_ref, xn_ref, *, scale, HPS):
    @pl.when(pl.program_id(1) == 0)
    def _():
        x = cq_ref[...]
        ms = jnp.mean(x * x, axis=-1, keepdims=True)
        xn_ref[...] = (x * lax.rsqrt(ms + NORM_EPS) * g_ref[...]).astype(BF16)

    acc = lax.dot_general(w_ref[0], xn_ref[...], _NT, preferred_element_type=F32)
    tm = acc.shape[1]
    half = MLA_ROPE // 2
    cos, sin = cos_ref[...], sin_ref[...]
    tail_row = lax.broadcasted_iota(jnp.int32, (MLA_QK_PAD - MLA_NOPE - MLA_ROPE, tm), 0)
    tail = jnp.where(tail_row == 0, 1.0, 0.0)
    for hh in range(HPS):
        qn = acc[hh * MLA_NOPE:(hh + 1) * MLA_NOPE, :]
        r0 = HPS * MLA_NOPE + hh * MLA_ROPE
        x1, x2 = acc[r0:r0 + half, :], acc[r0 + half:r0 + MLA_ROPE, :]
        qr = jnp.concatenate([x1 * cos - x2 * sin, x2 * cos + x1 * sin], axis=0)
        o_ref[hh] = jnp.concatenate([qn * scale, qr * scale, tail], axis=0).astype(o_ref.dtype)


def _kv_kernel(ckv_ref, g_ref, wk_ref, wvt_ref, kr_ref, cos_ref, sin_ref, k_ref, vt_ref, xn_ref,
               *, HPS, TK):
    @pl.when(pl.program_id(1) == 0)
    def _():
        x = ckv_ref[...]
        ms = jnp.mean(x * x, axis=-1, keepdims=True)
        xn_ref[...] = (x * lax.rsqrt(ms + NORM_EPS) * g_ref[...]).astype(BF16)

    xn = xn_ref[...]
    k = jnp.dot(xn, wk_ref[0], preferred_element_type=F32)
    tm = k.shape[0]
    kr = _rope(kr_ref[...][:, 0:MLA_ROPE], cos_ref[...], sin_ref[...])
    tail_shape = (tm, MLA_QK_PAD - MLA_NOPE - MLA_ROPE)
    rows = pl.program_id(0) * tm + lax.broadcasted_iota(jnp.int32, tail_shape, 0)
    lane = lax.broadcasted_iota(jnp.int32, tail_shape, 1)
    pad = jnp.where((lane == 0) & (rows < FRONT_PAD), NEG, 0.0)
    vt = lax.dot_general(wvt_ref[0], xn, _NT, preferred_element_type=F32)
    ones_row = lax.broadcasted_iota(jnp.int32, (MLA_VT_ROWS - MLA_V, tm), 0)
    ones = jnp.where(ones_row == 0, 1.0, 0.0)
    for hh in range(HPS):
        kh = jnp.concatenate([k[:, hh * MLA_NOPE:(hh + 1) * MLA_NOPE], kr, pad], axis=1).astype(k_ref.dtype)
        vh = jnp.concatenate([vt[hh * MLA_V:(hh + 1) * MLA_V, :], ones], axis=0).astype(vt_ref.dtype)
        for t in range(tm // TK):
            k_ref[hh, t] = kh[t * TK:(t + 1) * TK, :]
            vt_ref[hh, t] = vh[:, t * TK:(t + 1) * TK]


def _attn_kernel(q_ref, k_ref, vt_ref, gate_ref, o_ref, s_ref, mx_ref, m_ref, acc_ref, *, TK, NQ):
    iq = pl.program_id(1)
    k_row = lax.broadcasted_iota(jnp.int32, (TK, TK), 0)
    q_col = lax.broadcasted_iota(jnp.int32, (TK, TK), 1)
    diag_ok = (k_row // CHUNK) <= (q_col // CHUNK)

    def with_max(s, masked):
        if masked:
            s = jnp.where(diag_ok, s, NEG)
        return s, jnp.max(s, axis=0, keepdims=True)

    def scores(kt, c, masked=False):
        return with_max(jnp.dot(kt, q_ref[0, :, c * TK:(c + 1) * TK], preferred_element_type=F32), masked)

    def update(c, s_mx, vt):
        s, mx = s_mx
        m = m_ref[c]
        m_new = jnp.maximum(m, mx)
        alpha = jnp.exp2(m - m_new)
        p = jnp.exp2(s - m_new)
        m_ref[c] = m_new
        acc_ref[c] = alpha * acc_ref[c] + jnp.dot(vt, p.astype(BF16), preferred_element_type=F32)

    def stash(s_all):
        for c in range(NQ):
            s_ref[c], mx_ref[c] = s_all[c]

    def fetch():
        return [(s_ref[c], mx_ref[c]) for c in range(NQ)]

    m_ref[...] = jnp.full(m_ref.shape, NEG, F32)
    acc_ref[...] = jnp.zeros(acc_ref.shape, F32)
    n_full = NQ * iq
    stash([scores(k_ref[0, 0], c) for c in range(NQ)])

    def full_tile(j, s_now):
        kt_next = k_ref[0, j + 1]
        vt = vt_ref[0, j]
        s_next = [scores(kt_next, 0)]
        for c in range(NQ):
            if c + 1 < NQ:
                s_next.append(scores(kt_next, c + 1))
            update(c, s_now[c], vt)
        return s_next

    def multi_body(jj, _):
        s_now = fetch()
        for t in range(ATT_UNROLL):
            s_now = full_tile(ATT_UNROLL * jj + t, s_now)
        stash(s_now)
        return 0

    def single_body(j, _):
        stash(full_tile(j, fetch()))
        return 0

    n_multi = n_full // ATT_UNROLL
    lax.fori_loop(0, n_multi, multi_body, 0)
    lax.fori_loop(n_multi * ATT_UNROLL, n_full, single_body, 0)

    s_d = fetch()
    s_d[0] = with_max(s_d[0][0], True)
    for d in range(NQ):
        if d + 1 < NQ:
            kt = k_ref[0, n_full + d + 1]
            s_next = [None] * (d + 1) + [scores(kt, c, masked=(c == d + 1)) for c in range(d + 1, NQ)]
        vt = vt_ref[0, n_full + d]
        for c in range(d, NQ):
            update(c, s_d[c], vt)
        s_d = s_next
    for c in range(NQ):
        a = acc_ref[c]
        o = (a[0:MLA_V, :] * (1.0 / a[MLA_V:MLA_V + 1, :])).T
        rows = slice(c * TK, (c + 1) * TK)
        o_ref[rows, :] = (o * _silu(gate_ref[rows, :])).astype(o_ref.dtype)


def mla_mixer(proj, kr_raw, q_norm, w_uq, kv_norm, w_ukv, d_model, row_pos):
    lp = proj.shape[0]
    d = d_model
    h = d // MLA_V
    qr, kvr = d // 4, d // 8
    ta = ATT_ROWS
    nt = lp // ta
    half = MLA_ROPE // 2
    inv_freq = ROPE_BASE ** (-jnp.arange(0, MLA_ROPE, 2, dtype=F32) / MLA_ROPE)
    ang = row_pos[:, None] * inv_freq[None, :]
    cos, sin = jnp.cos(ang), jnp.sin(ang)

    hps = MLA_HPS
    ng = h // hps
    wq = w_uq.reshape(qr, ng, hps, MLA_NOPE + MLA_ROPE)
    wq = jnp.concatenate([wq[..., :MLA_NOPE].reshape(qr, ng, hps * MLA_NOPE),
                          wq[..., MLA_NOPE:].reshape(qr, ng, hps * MLA_ROPE)], axis=-1)
    wqt = wq.transpose(1, 2, 0).astype(BF16)
    wkv = w_ukv.reshape(kvr, ng, hps, MLA_NOPE + MLA_V)
    wk = wkv[..., :MLA_NOPE].reshape(kvr, ng, hps * MLA_NOPE).transpose(1, 0, 2).astype(BF16)
    wvt = wkv[..., MLA_NOPE:].reshape(kvr, ng, hps * MLA_V).transpose(1, 2, 0).astype(BF16)

    tmq = ATT_Q_TILES * ta
    scale = (MLA_NOPE + MLA_ROPE) ** -0.5 * math.log2(math.e)
    qt3 = pl.pallas_call(
        functools.partial(_q_kernel, scale=scale, HPS=hps),
        grid=(lp // tmq, ng),
        in_specs=[
            pl.BlockSpec((tmq, qr), lambda i, j: (i, 0)),
            pl.BlockSpec((1, qr), lambda i, j: (0, 0)),
            pl.BlockSpec((1, hps * (MLA_NOPE + MLA_ROPE), qr), lambda i, j: (j, 0, 0)),
            pl.BlockSpec((half, tmq), lambda i, j: (0, i)),
            pl.BlockSpec((half, tmq), lambda i, j: (0, i)),
        ],
        out_specs=pl.BlockSpec((hps, MLA_QK_PAD, tmq), lambda i, j: (j, 0, i)),
        out_shape=jax.ShapeDtypeStruct((h, MLA_QK_PAD, lp), BF16),
        scratch_shapes=[pltpu.VMEM((tmq, qr), BF16)],
        compiler_params=_cparams("parallel", "arbitrary"),
        name="mla_q",
    )(proj, q_norm.astype(F32).reshape(1, qr), wqt, cos.T, sin.T)

    tpk = tmq // ta
    k4, vt4 = pl.pallas_call(
        functools.partial(_kv_kernel, HPS=hps, TK=ta),
        grid=(lp // tmq, ng),
        in_specs=[
            pl.BlockSpec((tmq, kvr), lambda i, j: (i, qr // kvr)),
            pl.BlockSpec((1, kvr), lambda i, j: (0, 0)),
            pl.BlockSpec((1, kvr, hps * MLA_NOPE), lambda i, j: (j, 0, 0)),
            pl.BlockSpec((1, hps * MLA_V, kvr), lambda i, j: (j, 0, 0)),
            pl.BlockSpec((tmq, 128), lambda i, j: (i, 0)),
            pl.BlockSpec((tmq, half), lambda i, j: (i, 0)),
            pl.BlockSpec((tmq, half), lambda i, j: (i, 0)),
        ],
        out_specs=[pl.BlockSpec((hps, tpk, ta, MLA_QK_PAD), lambda i, j: (j, i, 0, 0)),
                   pl.BlockSpec((hps, tpk, MLA_VT_ROWS, ta), lambda i, j: (j, i, 0, 0))],
        out_shape=[jax.ShapeDtypeStruct((h, nt, ta, MLA_QK_PAD), BF16),
                   jax.ShapeDtypeStruct((h, nt, MLA_VT_ROWS, ta), BF16)],
        scratch_shapes=[pltpu.VMEM((tmq, kvr), BF16)],
        compiler_params=_cparams("parallel", "arbitrary"),
        name="mla_kv",
    )(proj, kv_norm.astype(F32).reshape(1, kvr), wk, wvt, kr_raw, cos, sin)

    gate_blk = (qr + kvr) // MLA_V
    nq = ATT_Q_TILES
    tq = nq * ta
    return pl.pallas_call(
        functools.partial(_attn_kernel, TK=ta, NQ=nq),
        grid=(h, lp // tq),
        in_specs=[
            pl.BlockSpec((1, MLA_QK_PAD, tq), lambda hh, i: (hh, 0, i)),
            pl.BlockSpec((1, nt, ta, MLA_QK_PAD), lambda hh, i: (hh, 0, 0, 0)),
            pl.BlockSpec((1, nt, MLA_VT_ROWS, ta), lambda hh, i: (hh, 0, 0, 0)),
            pl.BlockSpec((tq, MLA_V), lambda hh, i: (i, gate_blk + hh)),
        ],
        out_specs=pl.BlockSpec((tq, MLA_V), lambda hh, i: (i, hh)),
        out_shape=jax.ShapeDtypeStruct((lp, d), BF16),
        scratch_shapes=[pltpu.VMEM((nq, ta, ta), F32), pltpu.VMEM((nq, 1, ta), F32),
                        pltpu.VMEM((nq, 1, ta), F32), pltpu.VMEM((nq, MLA_VT_ROWS, ta), F32)],
        compiler_params=_cparams("parallel", "parallel"),
        name="mla_attention",
    )(qt3, k4, vt4, proj)


def _hybrid_layer(h, norm_g, w_in, conv_w, conv_b, dt_bias, a_log, ssd_d, ssd_norm, s5_ops, s5_d,
                  w_glu, w_out):
    lp, d = h.shape
    heads = d // SSD_HEADDIM
    e = heads // SSD_GROUPS
    conv_dim = d + 2 * SSD_GROUPS * SSD_STATE
    s5w = d // 2
    o_dt = d + conv_dim
    hn = rms_norm_rows(h, norm_g, BF16, NORM_ROWS)
    w_main = jnp.concatenate([w_in[:, :o_dt], w_in[:, o_dt + heads:]], axis=1).astype(BF16)
    w_dt = w_in[:, o_dt:o_dt + heads].reshape(d, SSD_GROUPS, e)
    w_dt = jnp.pad(w_dt, ((0, 0), (0, 0), (0, 128 // SSD_GROUPS - e))).reshape(d, 128).astype(BF16)
    proj = matmul([hn], [w_main], F32, name="hyb_in_proj")
    dt_raw = matmul([hn], [w_dt], F32, name="hyb_dt_proj")
    dtg = jnp.pad(dt_raw.reshape(lp, SSD_GROUPS, 128 // SSD_GROUPS),
                  ((0, 0), (0, 0), (0, 128 - 128 // SSD_GROUPS))).reshape(lp, SSD_GROUPS * 128)
    y_a = ssd_mixer(proj, dtg, conv_w, conv_b, dt_bias, a_log, ssd_d, ssd_norm, d)

    u_off = o_dt
    gate_off = o_dt + s5w
    g5 = s5w // S5_GROUP
    nb = lp // S5_BLOCK
    u_blk = proj[:, u_off:u_off + s5w].astype(BF16).reshape(nb, S5_BLOCK, g5, S5_GROUP)
    u_blk = u_blk.transpose(2, 0, 1, 3).reshape(g5, nb, S5_BLOCK * S5_GROUP)
    y_blk = s5_ssm(u_blk, s5_ops)
    y_ssm = y_blk.reshape(g5, nb, S5_BLOCK, S5_GROUP).transpose(1, 2, 0, 3).reshape(lp, s5w)
    y_b = s5_glu(y_ssm, proj, s5_d, w_glu, u_off, gate_off)
    w_o = w_out.astype(BF16)
    return matmul([y_a, y_b], [w_o[:d], w_o[d:]], F32, res=h, name="hyb_out_proj")


def _mla_layer(h, norm_g, w_in, q_norm, w_uq, kv_norm, w_ukv, w_out, row_pos):
    lp, d = h.shape
    qr, kvr = d // 4, d // 8
    hn = rms_norm_rows(h, norm_g, BF16, NORM_ROWS)
    o_kr = qr + kvr
    w_main = jnp.concatenate([w_in[:, :o_kr], w_in[:, o_kr + MLA_ROPE:]], axis=1).astype(BF16)
    w_kr = jnp.pad(w_in[:, o_kr:o_kr + MLA_ROPE], ((0, 0), (0, 128 - MLA_ROPE))).astype(BF16)
    proj = matmul([hn], [w_main], F32, name="mla_in_proj")
    kr_raw = matmul([hn], [w_kr], F32, name="mla_kr_proj")
    y_c = mla_mixer(proj, kr_raw, q_norm, w_uq, kv_norm, w_ukv, d, row_pos)
    return matmul([y_c], [w_out.astype(BF16)], F32, res=h, name="mla_out_proj")


def kernel(x, meta, hyb_norm, hyb_w_in, ssd_conv_w, ssd_conv_b, ssd_dt_bias, ssd_a_log, ssd_d, ssd_norm, s5_a_re, s5_a_im, s5_log_dt, s5_b_re, s5_b_im, s5_c_re, s5_c_im, s5_d, s5_w_glu, hyb_w_out, mla_norm, mla_w_in, mla_q_norm, mla_w_uq, mla_kv_norm, mla_w_ukv, mla_w_out, final_norm):
    bsz, seq, d = x.shape
    assert bsz == 1 and seq % CHUNK == 0 and meta.shape == (N_META, d)
    depth = hyb_norm.shape[0] + mla_norm.shape[0]
    first = FRONT_PAD + N_META
    lp = -(-(first + seq) // ROW_ALIGN) * ROW_ALIGN
    h = jnp.concatenate([jnp.zeros((FRONT_PAD, d), F32), meta.astype(F32), x[0].astype(F32),
                         jnp.zeros((lp - first - seq, d), F32)], axis=0)
    row_pos = jnp.arange(lp, dtype=jnp.int32).astype(F32) - float(FRONT_PAD)
    for layer in range(depth):
        i = layer // 2
        if layer % 2 == 0:
            ops = _s5_operators(s5_a_re[i], s5_a_im[i], s5_log_dt[i], s5_b_re[i], s5_b_im[i],
                                s5_c_re[i], s5_c_im[i])
            h = _hybrid_layer(h, hyb_norm[i], hyb_w_in[i], ssd_conv_w[i], ssd_conv_b[i],
                              ssd_dt_bias[i], ssd_a_log[i], ssd_d[i], ssd_norm[i], ops, s5_d[i],
                              s5_w_glu[i], hyb_w_out[i])
        else:
            h = _mla_layer(h, mla_norm[i], mla_w_in[i], mla_q_norm[i], mla_w_uq[i], mla_kv_norm[i],
                           mla_w_ukv[i], mla_w_out[i], row_pos)
    out = rms_norm_rows(h, final_norm, x.dtype, CHUNK, row_block_offset=first // CHUNK, n_rows=seq)
    return out.reshape(1, seq, d)
```

```python
import functools
import math

import jax
import jax.numpy as jnp
from jax import lax
from jax.experimental import pallas as pl
from jax.experimental.pallas import tpu as pltpu

F32 = jnp.float32
BF16 = jnp.bfloat16

CHUNK = 64
N_META = 16
FRONT_PAD = (-N_META) % CHUNK
NORM_EPS = 1e-6

SSD_HEADDIM = 64
SSD_GROUPS = 8
SSD_STATE = 128
SSD_CONV = 4

S5_GROUP = 16
S5_STATE = 64
S5_BLOCK = 16
S5_GB = 8

MLA_NOPE = 128
MLA_ROPE = 64
MLA_V = 128
MLA_QK_PAD = 256
MLA_VT_ROWS = MLA_V + 16
ROPE_BASE = 10000.0

ROW_ALIGN = 1280
MM_ROWS = 640
NORM_ROWS = 256
SSD_ROWS = 128
ATT_ROWS = 256
ATT_Q_TILES = 5
ATT_UNROLL = 8
MLA_HPS = 4
NEG = -1e30
VMEM_LIMIT = 56 * 1024 * 1024

_NT = (((1,), (1,)), ((), ()))


def _cparams(*sem):
    return pltpu.CompilerParams(dimension_semantics=sem, vmem_limit_bytes=VMEM_LIMIT)


def _silu(x):
    return x * (1.0 / (1.0 + jnp.exp(-x)))


def _sigmoid(x):
    return 1.0 / (1.0 + jnp.exp(-x))


def _softplus(x):
    return jnp.maximum(x, 0.0) + jnp.log(1.0 + jnp.exp(-jnp.abs(x)))


def _pick(n, cands):
    for c in cands:
        if n % c == 0:
            return c
    raise ValueError(f"no tile for {n} in {cands}")


def _rms_kernel(x_ref, g_ref, o_ref):
    x = x_ref[...]
    ms = jnp.mean(x * x, axis=-1, keepdims=True)
    o_ref[...] = (x * lax.rsqrt(ms + NORM_EPS) * g_ref[...]).astype(o_ref.dtype)


def rms_norm_rows(x, g, out_dtype, rows, row_block_offset=0, n_rows=None):
    n, d = x.shape
    n_rows = n if n_rows is None else n_rows
    return pl.pallas_call(
        _rms_kernel,
        grid=(n_rows // rows,),
        in_specs=[pl.BlockSpec((rows, d), lambda i: (i + row_block_offset, 0)),
                  pl.BlockSpec((1, d), lambda i: (0, 0))],
        out_specs=pl.BlockSpec((rows, d), lambda i: (i, 0)),
        out_shape=jax.ShapeDtypeStruct((n_rows, d), out_dtype),
        compiler_params=_cparams("parallel"),
        name="rms_norm",
    )(x, g.reshape(1, d).astype(F32))


def _mm_kernel(*refs, nx, has_res):
    o_ref = refs[-1]
    acc = None
    for xr, wr in zip(refs[:nx], refs[nx:2 * nx]):
        d = jnp.dot(xr[...], wr[...], preferred_element_type=F32)
        acc = d if acc is None else acc + d
    if has_res:
        acc = acc + refs[2 * nx][...]
    o_ref[...] = acc.astype(o_ref.dtype)


def matmul(xs, ws, out_dtype, res=None, name="matmul"):
    m = xs[0].shape[0]
    n = ws[0].shape[1]
    tm = _pick(m, (MM_ROWS,))
    tn = _pick(n, (512, 256, 128))
    in_specs = [pl.BlockSpec((tm, x.shape[1]), lambda i, j: (i, 0)) for x in xs]
    in_specs += [pl.BlockSpec((w.shape[0], tn), lambda i, j: (0, j)) for w in ws]
    args = list(xs) + list(ws)
    if res is not None:
        in_specs.append(pl.BlockSpec((tm, tn), lambda i, j: (i, j)))
        args.append(res)
    return pl.pallas_call(
        functools.partial(_mm_kernel, nx=len(xs), has_res=res is not None),
        grid=(m // tm, n // tn),
        in_specs=in_specs,
        out_specs=pl.BlockSpec((tm, tn), lambda i, j: (i, j)),
        out_shape=jax.ShapeDtypeStruct((m, n), out_dtype),
        compiler_params=_cparams("parallel", "parallel"),
        name=name,
    )(*args)


def _norm_proj_kernel(x_ref, g_ref, w_ref, ws_ref, o_ref, os_ref, xn_ref):
    @pl.when(pl.program_id(1) == 0)
    def _():
        x = x_ref[...]
        ms = jnp.mean(x * x, axis=-1, keepdims=True)
        xn_ref[...] = (x * lax.rsqrt(ms + NORM_EPS) * g_ref[...]).astype(BF16)
        os_ref[...] = jnp.dot(xn_ref[...], ws_ref[...], preferred_element_type=F32)

    o_ref[...] = jnp.dot(xn_ref[...], w_ref[...], preferred_element_type=F32)


def norm_proj(x, g, w_main, w_side, name):
    m, d = x.shape
    n = w_main.shape[1]
    tm = _pick(m, (MM_ROWS,))
    tn = _pick(n, (512, 256, 128))
    ns = w_side.shape[1]
    return pl.pallas_call(
        _norm_proj_kernel,
        grid=(m // tm, n // tn),
        in_specs=[pl.BlockSpec((tm, d), lambda i, j: (i, 0)),
                  pl.BlockSpec((1, d), lambda i, j: (0, 0)),
                  pl.BlockSpec((d, tn), lambda i, j: (0, j)),
                  pl.BlockSpec((d, ns), lambda i, j: (0, 0))],
        out_specs=[pl.BlockSpec((tm, tn), lambda i, j: (i, j)),
                   pl.BlockSpec((tm, ns), lambda i, j: (i, 0))],
        out_shape=[jax.ShapeDtypeStruct((m, n), F32), jax.ShapeDtypeStruct((m, ns), F32)],
        scratch_shapes=[pltpu.VMEM((tm, d), BF16)],
        compiler_params=_cparams("parallel", "arbitrary"),
        name=name,
    )(x, g.reshape(1, d).astype(F32), w_main, w_side)


def _ssd_kernel(z_ref, x_ref, b_ref, c_ref, dt_ref, cwx_ref, cwb_ref, cwc_ref, cbx_ref, cbb_ref,
                cbc_ref, dtb_ref, alog_ref, dskip_ref, ng_ref, o_ref, ext_ref, state_ref, *, T, E):
    c = pl.program_id(1)
    W = E * SSD_HEADDIM
    N = SSD_STATE

    @pl.when(c == 0)
    def _():
        ext_ref[0:8, :] = jnp.zeros((8, W + 2 * N), F32)
        state_ref[...] = jnp.zeros_like(state_ref)

    @pl.when(c > 0)
    def _():
        ext_ref[0:8, :] = ext_ref[T:T + 8, :]

    ext_ref[8:T + 8, 0:W] = x_ref[...]
    ext_ref[8:T + 8, W:W + N] = b_ref[...]
    ext_ref[8:T + 8, W + N:W + 2 * N] = c_ref[...]

    cw = jnp.concatenate([cwx_ref[...], cwb_ref[...], cwc_ref[...]], axis=1)
    acc = jnp.concatenate([cbx_ref[...], cbb_ref[...], cbc_ref[...]], axis=1)
    for k in range(SSD_CONV):
        acc = acc + cw[k:k + 1, :] * ext_ref[pl.ds(8 - (SSD_CONV - 1) + k, T), :]
    xc = _silu(acc)
    xs = xc[:, 0:W]
    bm = xc[:, W:W + N]
    cm = xc[:, W + N:W + 2 * N]

    rows = c * T + lax.broadcasted_iota(jnp.int32, (T, 128), 0)
    dtv = _softplus(dt_ref[...] + dtb_ref[...])
    dtv = jnp.where(rows >= FRONT_PAD, dtv, 0.0)
    a = -jnp.exp(alog_ref[...])
    da = dtv * a
    r_i = lax.broadcasted_iota(jnp.int32, (T, T), 0)
    c_i = lax.broadcasted_iota(jnp.int32, (T, T), 1)
    tril = r_i >= c_i
    acum = jnp.dot(tril.astype(F32), da, precision=lax.Precision.HIGHEST,
                   preferred_element_type=F32)
    acum_t = acum.T

    lane_t = lax.broadcasted_iota(jnp.int32, (T, 128), 1)
    lane_1 = lax.broadcasted_iota(jnp.int32, (1, 128), 1)

    def expand(v, lane):
        parts = [jnp.where(lane < SSD_HEADDIM, v[:, 2 * k:2 * k + 1], v[:, 2 * k + 1:2 * k + 2])
                 for k in range(E // 2)]
        return parts[0] if len(parts) == 1 else jnp.concatenate(parts, axis=1)

    xdt = xs * expand(dtv, lane_t)
    cb = lax.dot_general(cm.astype(BF16), bm.astype(BF16), _NT, preferred_element_type=F32)
    yd = []
    for e in range(E):
        seg = acum[:, e:e + 1] - acum_t[e:e + 1, :]
        g = (cb * jnp.exp(jnp.where(tril, seg, NEG))).astype(BF16)
        yd.append(jnp.dot(g, xdt[:, e * SSD_HEADDIM:(e + 1) * SSD_HEADDIM].astype(BF16),
                          preferred_element_type=F32))
    y_diag = jnp.concatenate(yd, axis=1)

    a_last = acum[T - 1:T, :]
    h_t = state_ref[...]
    y_off = jnp.dot(cm.astype(BF16), h_t.astype(BF16), preferred_element_type=F32)
    y_off = y_off * expand(jnp.exp(acum), lane_t)
    xw = (xdt * expand(jnp.exp(a_last - acum), lane_t)).astype(BF16)
    state_ref[...] = h_t * expand(jnp.exp(a_last), lane_1) + jnp.dot(
        bm.T.astype(BF16), xw, preferred_element_type=F32)

    y = y_diag + y_off + dskip_ref[...] * xs
    y = y * _silu(z_ref[...])
    ms = jnp.mean(y * y, axis=-1, keepdims=True)
    o_ref[...] = (y * lax.rsqrt(ms + NORM_EPS) * ng_ref[...]).astype(o_ref.dtype)


def ssd_mixer(proj, dtg, conv_w, conv_b, dt_bias, a_log, d_skip, norm_g, d_model):
    lp = proj.shape[0]
    d = d_model
    heads = d // SSD_HEADDIM
    e = heads // SSD_GROUPS
    w = d // SSD_GROUPS
    n = SSD_STATE
    t = SSD_ROWS
    xb, bb, cb = d // w, 2 * d // n, (2 * d + SSD_GROUPS * n) // n

    def padg(v):
        return jnp.pad(v.astype(F32).reshape(SSD_GROUPS, e), ((0, 0), (0, 128 - e))).reshape(1, -1)

    cw = conv_w.astype(F32)
    cbias = conv_b.astype(F32).reshape(1, -1)
    dsk = jnp.repeat(d_skip.astype(F32), SSD_HEADDIM).reshape(1, d)
    return pl.pallas_call(
        functools.partial(_ssd_kernel, T=t, E=e),
        grid=(SSD_GROUPS, lp // t),
        in_specs=[
            pl.BlockSpec((t, w), lambda g, c: (c, g)),
            pl.BlockSpec((t, w), lambda g, c: (c, xb + g)),
            pl.BlockSpec((t, n), lambda g, c: (c, bb + g)),
            pl.BlockSpec((t, n), lambda g, c: (c, cb + g)),
            pl.BlockSpec((t, 128), lambda g, c: (c, g)),
            pl.BlockSpec((SSD_CONV, w), lambda g, c: (0, g)),
            pl.BlockSpec((SSD_CONV, n), lambda g, c: (0, d // n + g)),
            pl.BlockSpec((SSD_CONV, n), lambda g, c: (0, (d + SSD_GROUPS * n) // n + g)),
            pl.BlockSpec((1, w), lambda g, c: (0, g)),
            pl.BlockSpec((1, n), lambda g, c: (0, d // n + g)),
            pl.BlockSpec((1, n), lambda g, c: (0, (d + SSD_GROUPS * n) // n + g)),
            pl.BlockSpec((1, 128), lambda g, c: (0, g)),
            pl.BlockSpec((1, 128), lambda g, c: (0, g)),
            pl.BlockSpec((1, w), lambda g, c: (0, g)),
            pl.BlockSpec((1, w), lambda g, c: (0, g)),
        ],
        out_specs=pl.BlockSpec((t, w), lambda g, c: (c, g)),
        out_shape=jax.ShapeDtypeStruct((lp, d), BF16),
        scratch_shapes=[pltpu.VMEM((t + 8, w + 2 * n), F32), pltpu.VMEM((n, w), F32)],
        compiler_params=_cparams("parallel", "arbitrary"),
        name="ssd_mixer",
    )(proj, proj, proj, proj, dtg, cw, cw, cw, cbias, cbias, cbias, padg(dt_bias), padg(a_log),
      dsk, norm_g.astype(F32).reshape(1, d))


def _s5_kernel(u_ref, trev_ref, bs_ref, cs_ref, ar_ref, ai_ref, y_ref, in_ref, sp_ref, st_ref, *, NBT):
    kb = S5_BLOCK
    half = S5_GB * S5_STATE

    @pl.when(pl.program_id(1) == 0)
    def _():
        st_ref[...] = jnp.zeros_like(st_ref)

    xcat = jnp.concatenate([u_ref[pl.ds(tt, NBT, stride=kb), :].astype(BF16) for tt in range(kb)], axis=1)
    in_ref[...] = jnp.dot(xcat, bs_ref[0], preferred_element_type=F32)

    ar = ar_ref[0]
    ai = ai_ref[0]

    def body(b, carry):
        re, im = carry
        row = pl.ds(b, 1)
        sp_ref[row, 0:half] = re
        sp_ref[row, half:2 * half] = im
        return (ar * re - ai * im + in_ref[row, 0:half], ar * im + ai * re + in_ref[row, half:2 * half])

    re, im = lax.fori_loop(0, NBT, body, (st_ref[0], st_ref[1]))
    st_ref[0] = re
    st_ref[1] = im

    y_state = jnp.dot(sp_ref[...].astype(BF16), cs_ref[0], preferred_element_type=F32)
    lanes = S5_GB * S5_GROUP
    for tt in range(kb):
        y = jnp.dot(xcat[:, 0:(tt + 1) * lanes], trev_ref[0, (kb - 1 - tt) * lanes:kb * lanes, :],
                    preferred_element_type=F32)
        y_ref[pl.ds(tt, NBT, stride=kb), :] = y + y_state[:, tt * lanes:(tt + 1) * lanes]


def _s5_operators(a_re, a_im, log_dt, b_re, b_im, c_re, c_im):
    hi = lax.Precision.HIGHEST
    g, p = a_re.shape
    kb = S5_BLOCK
    a_re, a_im = a_re.astype(F32), a_im.astype(F32)
    b_re, b_im = b_re.astype(F32), b_im.astype(F32)
    c_re, c_im = c_re.astype(F32), c_im.astype(F32)
    dt = jnp.exp(log_dt.astype(F32))[:, None]
    mag = jnp.exp(dt * a_re)
    ab_re, ab_im = mag * jnp.cos(dt * a_im), mag * jnp.sin(dt * a_im)
    den = a_re * a_re + a_im * a_im
    k_re = ((ab_re - 1.0) * a_re + ab_im * a_im) / den
    k_im = (ab_im * a_re - (ab_re - 1.0) * a_im) / den
    bb_re = k_re[..., None] * b_re - k_im[..., None] * b_im
    bb_im = k_re[..., None] * b_im + k_im[..., None] * b_re
    ks = jnp.arange(kb + 1, dtype=F32)[:, None, None]
    pm = jnp.exp(ks * (dt * a_re))
    pr, pi = pm * jnp.cos(ks * (dt * a_im)), pm * jnp.sin(ks * (dt * a_im))
    abr = pr[..., None] * bb_re - pi[..., None] * bb_im
    abi = pr[..., None] * bb_im + pi[..., None] * bb_re
    mk = (jnp.einsum('gjp,kgpi->kgji', c_re, abr[:kb], precision=hi)
          - jnp.einsum('gjp,kgpi->kgji', c_im, abi[:kb], precision=hi))
    car = c_re[None] * pr[1:, :, None, :] - c_im[None] * pi[1:, :, None, :]
    cai = c_re[None] * pi[1:, :, None, :] + c_im[None] * pr[1:, :, None, :]
    gb = S5_GB
    nblk = g // gb
    eye = jnp.eye(gb, dtype=F32)

    def blk(x):
        return x.reshape(x.shape[0], nblk, gb, *x.shape[2:])

    lanes = gb * S5_GROUP
    trev = jnp.einsum('rGlji,lm->Grlimj', blk(mk[::-1]), eye).reshape(nblk, kb * lanes, lanes)
    bs = jnp.concatenate(
        [jnp.einsum('tGlpi,lm->Gtlimp', blk(x[:kb][::-1]), eye).reshape(nblk, kb * lanes, gb * p)
         for x in (abr, abi)], axis=-1)
    cs = jnp.concatenate(
        [jnp.einsum('tGljp,lm->Glptmj', blk(x), eye).reshape(nblk, gb * p, kb * lanes)
         for x in (car, -cai)], axis=1)
    ar = pr[kb].reshape(nblk, 1, gb * p)
    ai = pi[kb].reshape(nblk, 1, gb * p)
    return trev.astype(BF16), bs.astype(BF16), cs.astype(BF16), ar, ai


def s5_ssm(proj, ops, u_off, s5w):
    trev, bs, cs, ar, ai = ops
    lp = proj.shape[0]
    lanes = S5_GB * S5_GROUP
    states = 2 * S5_GB * S5_STATE
    nblk = s5w // lanes
    nb = lp // S5_BLOCK
    nbt = _pick(nb, (208, 40))
    rows = nbt * S5_BLOCK
    return pl.pallas_call(
        functools.partial(_s5_kernel, NBT=nbt),
        grid=(nblk, nb // nbt),
        in_specs=[
            pl.BlockSpec((rows, lanes), lambda i, j: (j, u_off // lanes + i)),
            pl.BlockSpec((1, S5_BLOCK * lanes, lanes), lambda i, j: (i, 0, 0)),
            pl.BlockSpec((1, S5_BLOCK * lanes, states), lambda i, j: (i, 0, 0)),
            pl.BlockSpec((1, states, S5_BLOCK * lanes), lambda i, j: (i, 0, 0)),
            pl.BlockSpec((1, 1, states // 2), lambda i, j: (i, 0, 0)),
            pl.BlockSpec((1, 1, states // 2), lambda i, j: (i, 0, 0)),
        ],
        out_specs=pl.BlockSpec((rows, lanes), lambda i, j: (j, i)),
        out_shape=jax.ShapeDtypeStruct((lp, s5w), F32),
        scratch_shapes=[pltpu.VMEM((nbt, states), F32), pltpu.VMEM((nbt, states), F32),
                        pltpu.VMEM((2, 1, states // 2), F32)],
        compiler_params=_cparams("parallel", "arbitrary"),
        name="s5_ssm",
    )(proj, trev, bs, cs, ar, ai)


def _glu_kernel(y_ref, u_ref, gate_ref, d_ref, w_ref, o_ref):
    y = y_ref[...] + d_ref[...] * u_ref[...]
    c0 = math.sqrt(2.0 / math.pi)
    g = 0.5 * y * (1.0 + jnp.tanh(c0 * (y + 0.044715 * (y * y * y))))
    lin = jnp.dot(g.astype(BF16), w_ref[...], preferred_element_type=F32)
    o_ref[...] = (g * _sigmoid(lin) * _silu(gate_ref[...])).astype(o_ref.dtype)


def s5_glu(y_ssm, proj, d_skip, w_glu, u_off, gate_off):
    lp, sw = y_ssm.shape
    tm = _pick(lp, (MM_ROWS,))
    return pl.pallas_call(
        _glu_kernel,
        grid=(lp // tm,),
        in_specs=[
            pl.BlockSpec((tm, sw), lambda i: (i, 0)),
            pl.BlockSpec((tm, sw), lambda i: (i, u_off // sw)),
            pl.BlockSpec((tm, sw), lambda i: (i, gate_off // sw)),
            pl.BlockSpec((1, sw), lambda i: (0, 0)),
            pl.BlockSpec((sw, sw), lambda i: (0, 0)),
        ],
        out_specs=pl.BlockSpec((tm, sw), lambda i: (i, 0)),
        out_shape=jax.ShapeDtypeStruct((lp, sw), BF16),
        compiler_params=_cparams("parallel"),
        name="s5_glu",
    )(y_ssm, proj, proj, d_skip.astype(F32).reshape(1, sw), w_glu.astype(BF16))


def _rope(r, cos, sin):
    half = MLA_ROPE // 2
    x1, x2 = r[:, :half], r[:, half:]
    return jnp.concatenate([x1 * cos - x2 * sin, x2 * cos + x1 * sin], axis=1)


def _q_kernel(cq_ref, g_ref, w_ref, cos_ref, sin_ref, o_ref, xn_ref, *, scale, HPS):
    @pl.when(pl.program_id(1) == 0)
    def _():
        x = cq_ref[...]
        ms = jnp.mean(x * x, axis=-1, keepdims=True)
        xn_ref[...] = (x * lax.rsqrt(ms + NORM_EPS) * g_ref[...]).astype(BF16)

    acc = lax.dot_general(w_ref[0], xn_ref[...], _NT, preferred_element_type=F32)
    tm = acc.shape[1]
    half = MLA_ROPE // 2
    cos, sin = cos_ref[...], sin_ref[...]
    tail_row = lax.broadcasted_iota(jnp.int32, (MLA_QK_PAD - MLA_NOPE - MLA_ROPE, tm), 0)
    tail = jnp.where(tail_row == 0, 1.0, 0.0)
    for hh in range(HPS):
        qn = acc[hh * MLA_NOPE:(hh + 1) * MLA_NOPE, :]
        r0 = HPS * MLA_NOPE + hh * MLA_ROPE
        x1, x2 = acc[r0:r0 + half, :], acc[r0 + half:r0 + MLA_ROPE, :]
        qr = jnp.concatenate([x1 * cos - x2 * sin, x2 * cos + x1 * sin], axis=0)
        o_ref[hh] = jnp.concatenate([qn * scale, qr * scale, tail], axis=0).astype(o_ref.dtype)


def _kv_kernel(ckv_ref, g_ref, wk_ref, wvt_ref, kr_ref, cos_ref, sin_ref, k_ref, vt_ref, xn_ref,
               *, HPS, TK):
    @pl.when(pl.program_id(1) == 0)
    def _():
        x = ckv_ref[...]
        ms = jnp.mean(x * x, axis=-1, keepdims=True)
        xn_ref[...] = (x * lax.rsqrt(ms + NORM_EPS) * g_ref[...]).astype(BF16)

    xn = xn_ref[...]
    k = jnp.dot(xn, wk_ref[0], preferred_element_type=F32)
    tm = k.shape[0]
    kr = _rope(kr_ref[...][:, 0:MLA_ROPE], cos_ref[...], sin_ref[...])
    tail_shape = (tm, MLA_QK_PAD - MLA_NOPE - MLA_ROPE)
    rows = pl.program_id(0) * tm + lax.broadcasted_iota(jnp.int32, tail_shape, 0)
    lane = lax.broadcasted_iota(jnp.int32, tail_shape, 1)
    pad = jnp.where((lane == 0) & (rows < FRONT_PAD), NEG, 0.0)
    vt = lax.dot_general(wvt_ref[0], xn, _NT, preferred_element_type=F32)
    ones_row = lax.broadcasted_iota(jnp.int32, (MLA_VT_ROWS - MLA_V, tm), 0)
    ones = jnp.where(ones_row == 0, 1.0, 0.0)
    for hh in range(HPS):
        kh = jnp.concatenate([k[:, hh * MLA_NOPE:(hh + 1) * MLA_NOPE], kr, pad], axis=1).astype(k_ref.dtype)
        vh = jnp.concatenate([vt[hh * MLA_V:(hh + 1) * MLA_V, :], ones], axis=0).astype(vt_ref.dtype)
        for t in range(tm // TK):
            k_ref[hh, t] = kh[t * TK:(t + 1) * TK, :]
            vt_ref[hh, t] = vh[:, t * TK:(t + 1) * TK]


def _attn_kernel(q_ref, k_ref, vt_ref, gate_ref, o_ref, s_ref, mx_ref, m_ref, acc_ref, *, TK, NQ):
    iq = pl.program_id(1)
    k_row = lax.broadcasted_iota(jnp.int32, (TK, TK), 0)
    q_col = lax.broadcasted_iota(jnp.int32, (TK, TK), 1)
    diag_ok = (k_row // CHUNK) <= (q_col // CHUNK)

    def with_max(s, masked):
        if masked:
            s = jnp.where(diag_ok, s, NEG)
        return s, jnp.max(s, axis=0, keepdims=True)

    def scores(kt, c, masked=False):
        return with_max(jnp.dot(kt, q_ref[0, :, c * TK:(c + 1) * TK], preferred_element_type=F32), masked)

    def update(c, s_mx, vt):
        s, mx = s_mx
        m = m_ref[c]
        m_new = jnp.maximum(m, mx)
        alpha = jnp.exp2(m - m_new)
        p = jnp.exp2(s - m_new)
        m_ref[c] = m_new
        acc_ref[c] = alpha * acc_ref[c] + jnp.dot(vt, p.astype(BF16), preferred_element_type=F32)

    def stash(s_all):
        for c in range(NQ):
            s_ref[c], mx_ref[c] = s_all[c]

    def fetch():
        return [(s_ref[c], mx_ref[c]) for c in range(NQ)]

    m_ref[...] = jnp.full(m_ref.shape, NEG, F32)
    acc_ref[...] = jnp.zeros(acc_ref.shape, F32)
    n_full = NQ * iq
    stash([scores(k_ref[0, 0], c) for c in range(NQ)])

    def full_tile(j, s_now):
        kt_next = k_ref[0, j + 1]
        vt = vt_ref[0, j]
        s_next = [scores(kt_next, 0)]
        for c in range(NQ):
            if c + 1 < NQ:
                s_next.append(scores(kt_next, c + 1))
            update(c, s_now[c], vt)
        return s_next

    def multi_body(jj, _):
        s_now = fetch()
        for t in range(ATT_UNROLL):
            s_now = full_tile(ATT_UNROLL * jj + t, s_now)
        stash(s_now)
        return 0

    def single_body(j, _):
        stash(full_tile(j, fetch()))
        return 0

    n_multi = n_full // ATT_UNROLL
    lax.fori_loop(0, n_multi, multi_body, 0)
    lax.fori_loop(n_multi * ATT_UNROLL, n_full, single_body, 0)

    s_d = fetch()
    s_d[0] = with_max(s_d[0][0], True)
    for d in range(NQ):
        if d + 1 < NQ:
            kt = k_ref[0, n_full + d + 1]
            s_next = [None] * (d + 1) + [scores(kt, c, masked=(c == d + 1)) for c in range(d + 1, NQ)]
        vt = vt_ref[0, n_full + d]
        for c in range(d, NQ):
            update(c, s_d[c], vt)
        s_d = s_next
    for c in range(NQ):
        a = acc_ref[c]
        o = (a[0:MLA_V, :] * (1.0 / a[MLA_V:MLA_V + 1, :])).T
        rows = slice(c * TK, (c + 1) * TK)
        o_ref[rows, :] = (o * _silu(gate_ref[rows, :])).astype(o_ref.dtype)


def mla_mixer(proj, kr_raw, q_norm, w_uq, kv_norm, w_ukv, d_model, row_pos):
    lp = proj.shape[0]
    d = d_model
    h = d // MLA_V
    qr, kvr = d // 4, d // 8
    ta = ATT_ROWS
    nt = lp // ta
    half = MLA_ROPE // 2
    inv_freq = ROPE_BASE ** (-jnp.arange(0, MLA_ROPE, 2, dtype=F32) / MLA_ROPE)
    ang = row_pos[:, None] * inv_freq[None, :]
    cos, sin = jnp.cos(ang), jnp.sin(ang)

    hps = MLA_HPS
    ng = h // hps
    wq = w_uq.reshape(qr, ng, hps, MLA_NOPE + MLA_ROPE)
    wq = jnp.concatenate([wq[..., :MLA_NOPE].reshape(qr, ng, hps * MLA_NOPE),
                          wq[..., MLA_NOPE:].reshape(qr, ng, hps * MLA_ROPE)], axis=-1)
    wqt = wq.transpose(1, 2, 0).astype(BF16)
    wkv = w_ukv.reshape(kvr, ng, hps, MLA_NOPE + MLA_V)
    wk = wkv[..., :MLA_NOPE].reshape(kvr, ng, hps * MLA_NOPE).transpose(1, 0, 2).astype(BF16)
    wvt = wkv[..., MLA_NOPE:].reshape(kvr, ng, hps * MLA_V).transpose(1, 2, 0).astype(BF16)

    tmq = ATT_Q_TILES * ta
    scale = (MLA_NOPE + MLA_ROPE) ** -0.5 * math.log2(math.e)
    qt3 = pl.pallas_call(
        functools.partial(_q_kernel, scale=scale, HPS=hps),
        grid=(lp // tmq, ng),
        in_specs=[
            pl.BlockSpec((tmq, qr), lambda i, j: (i, 0)),
            pl.BlockSpec((1, qr), lambda i, j: (0, 0)),
            pl.BlockSpec((1, hps * (MLA_NOPE + MLA_ROPE), qr), lambda i, j: (j, 0, 0)),
            pl.BlockSpec((half, tmq), lambda i, j: (0, i)),
            pl.BlockSpec((half, tmq), lambda i, j: (0, i)),
        ],
        out_specs=pl.BlockSpec((hps, MLA_QK_PAD, tmq), lambda i, j: (j, 0, i)),
        out_shape=jax.ShapeDtypeStruct((h, MLA_QK_PAD, lp), BF16),
        scratch_shapes=[pltpu.VMEM((tmq, qr), BF16)],
        compiler_params=_cparams("parallel", "arbitrary"),
        name="mla_q",
    )(proj, q_norm.astype(F32).reshape(1, qr), wqt, cos.T, sin.T)

    tpk = tmq // ta
    k4, vt4 = pl.pallas_call(
        functools.partial(_kv_kernel, HPS=hps, TK=ta),
        grid=(lp // tmq, ng),
        in_specs=[
            pl.BlockSpec((tmq, kvr), lambda i, j: (i, qr // kvr)),
            pl.BlockSpec((1, kvr), lambda i, j: (0, 0)),
            pl.BlockSpec((1, kvr, hps * MLA_NOPE), lambda i, j: (j, 0, 0)),
            pl.BlockSpec((1, hps * MLA_V, kvr), lambda i, j: (j, 0, 0)),
            pl.BlockSpec((tmq, 128), lambda i, j: (i, 0)),
            pl.BlockSpec((tmq, half), lambda i, j: (i, 0)),
            pl.BlockSpec((tmq, half), lambda i, j: (i, 0)),
        ],
        out_specs=[pl.BlockSpec((hps, tpk, ta, MLA_QK_PAD), lambda i, j: (j, i, 0, 0)),
                   pl.BlockSpec((hps, tpk, MLA_VT_ROWS, ta), lambda i, j: (j, i, 0, 0))],
        out_shape=[jax.ShapeDtypeStruct((h, nt, ta, MLA_QK_PAD), BF16),
                   jax.ShapeDtypeStruct((h, nt, MLA_VT_ROWS, ta), BF16)],
        scratch_shapes=[pltpu.VMEM((tmq, kvr), BF16)],
        compiler_params=_cparams("parallel", "arbitrary"),
        name="mla_kv",
    )(proj, kv_norm.astype(F32).reshape(1, kvr), wk, wvt, kr_raw, cos, sin)

    gate_blk = (qr + kvr) // MLA_V
    nq = ATT_Q_TILES
    tq = nq * ta
    return pl.pallas_call(
        functools.partial(_attn_kernel, TK=ta, NQ=nq),
        grid=(h, lp // tq),
        in_specs=[
            pl.BlockSpec((1, MLA_QK_PAD, tq), lambda hh, i: (hh, 0, i)),
            pl.BlockSpec((1, nt, ta, MLA_QK_PAD), lambda hh, i: (hh, 0, 0, 0)),
            pl.BlockSpec((1, nt, MLA_VT_ROWS, ta), lambda hh, i: (hh, 0, 0, 0)),
            pl.BlockSpec((tq, MLA_V), lambda hh, i: (i, gate_blk + hh)),
        ],
        out_specs=pl.BlockSpec((tq, MLA_V), lambda hh, i: (i, hh)),
        out_shape=jax.ShapeDtypeStruct((lp, d), BF16),
        scratch_shapes=[pltpu.VMEM((nq, ta, ta), F32), pltpu.VMEM((nq, 1, ta), F32),
                        pltpu.VMEM((nq, 1, ta), F32), pltpu.VMEM((nq, MLA_VT_ROWS, ta), F32)],
        compiler_params=_cparams("parallel", "parallel"),
        name="mla_attention",
    )(qt3, k4, vt4, proj)


def _hybrid_layer(h, norm_g, w_in, conv_w, conv_b, dt_bias, a_log, ssd_d, ssd_norm, s5_ops, s5_d,
                  w_glu, w_out):
    lp, d = h.shape
    heads = d // SSD_HEADDIM
    e = heads // SSD_GROUPS
    conv_dim = d + 2 * SSD_GROUPS * SSD_STATE
    s5w = d // 2
    o_dt = d + conv_dim
    w_main = jnp.concatenate([w_in[:, :o_dt], w_in[:, o_dt + heads:]], axis=1).astype(BF16)
    w_dt = w_in[:, o_dt:o_dt + heads].reshape(d, SSD_GROUPS, e)
    w_dt = jnp.pad(w_dt, ((0, 0), (0, 0), (0, 128 // SSD_GROUPS - e))).reshape(d, 128).astype(BF16)
    proj, dt_raw = norm_proj(h, norm_g, w_main, w_dt, "hyb_in_proj")
    dtg = jnp.pad(dt_raw.reshape(lp, SSD_GROUPS, 128 // SSD_GROUPS),
                  ((0, 0), (0, 0), (0, 128 - 128 // SSD_GROUPS))).reshape(lp, SSD_GROUPS * 128)
    y_a = ssd_mixer(proj, dtg, conv_w, conv_b, dt_bias, a_log, ssd_d, ssd_norm, d)

    u_off = o_dt
    gate_off = o_dt + s5w
    y_ssm = s5_ssm(proj, s5_ops, u_off, s5w)
    y_b = s5_glu(y_ssm, proj, s5_d, w_glu, u_off, gate_off)
    w_o = w_out.astype(BF16)
    return matmul([y_a, y_b], [w_o[:d], w_o[d:]], F32, res=h, name="hyb_out_proj")


def _mla_layer(h, norm_g, w_in, q_norm, w_uq, kv_norm, w_ukv, w_out, row_pos):
    lp, d = h.shape
    qr, kvr = d // 4, d // 8
    o_kr = qr + kvr
    w_main = jnp.concatenate([w_in[:, :o_kr], w_in[:, o_kr + MLA_ROPE:]], axis=1).astype(BF16)
    w_kr = jnp.pad(w_in[:, o_kr:o_kr + MLA_ROPE], ((0, 0), (0, 128 - MLA_ROPE))).astype(BF16)
    proj, kr_raw = norm_proj(h, norm_g, w_main, w_kr, "mla_in_proj")
    y_c = mla_mixer(proj, kr_raw, q_norm, w_uq, kv_norm, w_ukv, d, row_pos)
    return matmul([y_c], [w_out.astype(BF16)], F32, res=h, name="mla_out_proj")


def kernel(x, meta, hyb_norm, hyb_w_in, ssd_conv_w, ssd_conv_b, ssd_dt_bias, ssd_a_log, ssd_d, ssd_norm, s5_a_re, s5_a_im, s5_log_dt, s5_b_re, s5_b_im, s5_c_re, s5_c_im, s5_d, s5_w_glu, hyb_w_out, mla_norm, mla_w_in, mla_q_norm, mla_w_uq, mla_kv_norm, mla_w_ukv, mla_w_out, final_norm):
    bsz, seq, d = x.shape
    assert bsz == 1 and seq % CHUNK == 0 and meta.shape == (N_META, d)
    depth = hyb_norm.shape[0] + mla_norm.shape[0]
    first = FRONT_PAD + N_META
    lp = -(-(first + seq) // ROW_ALIGN) * ROW_ALIGN
    h = jnp.concatenate([jnp.zeros((FRONT_PAD, d), F32), meta.astype(F32), x[0].astype(F32),
                         jnp.zeros((lp - first - seq, d), F32)], axis=0)
    row_pos = jnp.arange(lp, dtype=jnp.int32).astype(F32) - float(FRONT_PAD)
    for layer in range(depth):
        i = layer // 2
        if layer % 2 == 0:
            ops = _s5_operators(s5_a_re[i], s5_a_im[i], s5_log_dt[i], s5_b_re[i], s5_b_im[i],
                                s5_c_re[i], s5_c_im[i])
            h = _hybrid_layer(h, hyb_norm[i], hyb_w_in[i], ssd_conv_w[i], ssd_conv_b[i],
                              ssd_dt_bias[i], ssd_a_log[i], ssd_d[i], ssd_norm[i], ops, s5_d[i],
                              s5_w_glu[i], hyb_w_out[i])
        else:
            h = _mla_layer(h, mla_norm[i], mla_w_in[i], mla_q_norm[i], mla_w_uq[i], mla_kv_norm[i],
                           mla_w_ukv[i], mla_w_out[i], row_pos)
    out = rms_norm_rows(h, final_norm, x.dtype, CHUNK, row_block_offset=first // CHUNK, n_rows=seq)
    return out.reshape(1, seq, d)
```

```python
import functools
import math

import jax
import jax.numpy as jnp
from jax import lax
from jax.experimental import pallas as pl
from jax.experimental.pallas import tpu as pltpu

F32 = jnp.float32
BF16 = jnp.bfloat16

CHUNK = 64
N_META = 16
FRONT_PAD = (-N_META) % CHUNK
NORM_EPS = 1e-6

SSD_HEADDIM = 64
SSD_GROUPS = 8
SSD_STATE = 128
SSD_CONV = 4

S5_GROUP = 16
S5_STATE = 64
S5_BLOCK = 16
S5_GB = 8

MLA_NOPE = 128
MLA_ROPE = 64
MLA_V = 128
MLA_QK_PAD = 256
MLA_VT_ROWS = MLA_V + 16
ROPE_BASE = 10000.0

ROW_ALIGN = 1280
MM_ROWS = 640
NORM_ROWS = 256
SSD_ROWS = 128
ATT_ROWS = 256
ATT_Q_TILES = 5
ATT_UNROLL = 8
MLA_HPS = 4
NEG = -1e30
VMEM_LIMIT = 56 * 1024 * 1024

_NT = (((1,), (1,)), ((), ()))


def _cparams(*sem):
    return pltpu.CompilerParams(dimension_semantics=sem, vmem_limit_bytes=VMEM_LIMIT)


def _silu(x):
    return x * (1.0 / (1.0 + jnp.exp(-x)))


def _sigmoid(x):
    return 1.0 / (1.0 + jnp.exp(-x))


def _softplus(x):
    return jnp.maximum(x, 0.0) + jnp.log(1.0 + jnp.exp(-jnp.abs(x)))


def _pick(n, cands):
    for c in cands:
        if n % c == 0:
            return c
    raise ValueError(f"no tile for {n} in {cands}")


def _rms_kernel(x_ref, g_ref, o_ref):
    x = x_ref[...]
    ms = jnp.mean(x * x, axis=-1, keepdims=True)
    o_ref[...] = (x * lax.rsqrt(ms + NORM_EPS) * g_ref[...]).astype(o_ref.dtype)


def rms_norm_rows(x, g, out_dtype, rows, row_block_offset=0, n_rows=None):
    n, d = x.shape
    n_rows = n if n_rows is None else n_rows
    return pl.pallas_call(
        _rms_kernel,
        grid=(n_rows // rows,),
        in_specs=[pl.BlockSpec((rows, d), lambda i: (i + row_block_offset, 0)),
                  pl.BlockSpec((1, d), lambda i: (0, 0))],
        out_specs=pl.BlockSpec((rows, d), lambda i: (i, 0)),
        out_shape=jax.ShapeDtypeStruct((n_rows, d), out_dtype),
        compiler_params=_cparams("parallel"),
        name="rms_norm",
    )(x, g.reshape(1, d).astype(F32))


def _mm_kernel(*refs, nx, has_res):
    o_ref = refs[-1]
    acc = None
    for xr, wr in zip(refs[:nx], refs[nx:2 * nx]):
        d = jnp.dot(xr[...], wr[...], preferred_element_type=F32)
        acc = d if acc is None else acc + d
    if has_res:
        acc = acc + refs[2 * nx][...]
    o_ref[...] = acc.astype(o_ref.dtype)


def matmul(xs, ws, out_dtype, res=None, name="matmul"):
    m = xs[0].shape[0]
    n = ws[0].shape[1]
    tm = _pick(m, (MM_ROWS,))
    tn = _pick(n, (512, 256, 128))
    in_specs = [pl.BlockSpec((tm, x.shape[1]), lambda i, j: (i, 0)) for x in xs]
    in_specs += [pl.BlockSpec((w.shape[0], tn), lambda i, j: (0, j)) for w in ws]
    args = list(xs) + list(ws)
    if res is not None:
        in_specs.append(pl.BlockSpec((tm, tn), lambda i, j: (i, j)))
        args.append(res)
    return pl.pallas_call(
        functools.partial(_mm_kernel, nx=len(xs), has_res=res is not None),
        grid=(m // tm, n // tn),
        in_specs=in_specs,
        out_specs=pl.BlockSpec((tm, tn), lambda i, j: (i, j)),
        out_shape=jax.ShapeDtypeStruct((m, n), out_dtype),
        compiler_params=_cparams("parallel", "parallel"),
        name=name,
    )(*args)


def _norm_proj_kernel(x_ref, g_ref, w_ref, ws_ref, o_ref, os_ref, xn_ref):
    @pl.when(pl.program_id(1) == 0)
    def _():
        x = x_ref[...]
        ms = jnp.mean(x * x, axis=-1, keepdims=True)
        xn_ref[...] = (x * lax.rsqrt(ms + NORM_EPS) * g_ref[...]).astype(BF16)
        os_ref[...] = jnp.dot(xn_ref[...], ws_ref[...], preferred_element_type=F32)

    o_ref[...] = jnp.dot(xn_ref[...], w_ref[...], preferred_element_type=F32)


def norm_proj(x, g, w_main, w_side, name):
    m, d = x.shape
    n = w_main.shape[1]
    tm = _pick(m, (MM_ROWS,))
    tn = _pick(n, (512, 256, 128))
    ns = w_side.shape[1]
    return pl.pallas_call(
        _norm_proj_kernel,
        grid=(m // tm, n // tn),
        in_specs=[pl.BlockSpec((tm, d), lambda i, j: (i, 0)),
                  pl.BlockSpec((1, d), lambda i, j: (0, 0)),
                  pl.BlockSpec((d, tn), lambda i, j: (0, j)),
                  pl.BlockSpec((d, ns), lambda i, j: (0, 0))],
        out_specs=[pl.BlockSpec((tm, tn), lambda i, j: (i, j)),
                   pl.BlockSpec((tm, ns), lambda i, j: (i, 0))],
        out_shape=[jax.ShapeDtypeStruct((m, n), F32), jax.ShapeDtypeStruct((m, ns), F32)],
        scratch_shapes=[pltpu.VMEM((tm, d), BF16)],
        compiler_params=_cparams("parallel", "arbitrary"),
        name=name,
    )(x, g.reshape(1, d).astype(F32), w_main, w_side)


def _ssd_kernel(z_ref, x_ref, b_ref, c_ref, dt_ref, cwx_ref, cwb_ref, cwc_ref, cbx_ref, cbb_ref,
                cbc_ref, dtb_ref, alog_ref, dskip_ref, ng_ref, o_ref, ext_ref, state_ref, *, T, E):
    c = pl.program_id(1)
    W = E * SSD_HEADDIM
    N = SSD_STATE

    @pl.when(c == 0)
    def _():
        ext_ref[0:8, :] = jnp.zeros((8, W + 2 * N), F32)
        state_ref[...] = jnp.zeros_like(state_ref)

    @pl.when(c > 0)
    def _():
        ext_ref[0:8, :] = ext_ref[T:T + 8, :]

    ext_ref[8:T + 8, 0:W] = x_ref[...]
    ext_ref[8:T + 8, W:W + N] = b_ref[...]
    ext_ref[8:T + 8, W + N:W + 2 * N] = c_ref[...]

    cw = jnp.concatenate([cwx_ref[...], cwb_ref[...], cwc_ref[...]], axis=1)
    acc = jnp.concatenate([cbx_ref[...], cbb_ref[...], cbc_ref[...]], axis=1)
    for k in range(SSD_CONV):
        acc = acc + cw[k:k + 1, :] * ext_ref[pl.ds(8 - (SSD_CONV - 1) + k, T), :]
    xc = _silu(acc)
    xs = xc[:, 0:W]
    bm = xc[:, W:W + N]
    cm = xc[:, W + N:W + 2 * N]

    rows = c * T + lax.broadcasted_iota(jnp.int32, (T, 128), 0)
    dtv = _softplus(dt_ref[...] + dtb_ref[...])
    dtv = jnp.where(rows >= FRONT_PAD, dtv, 0.0)
    a = -jnp.exp(alog_ref[...])
    da = dtv * a
    r_i = lax.broadcasted_iota(jnp.int32, (T, T), 0)
    c_i = lax.broadcasted_iota(jnp.int32, (T, T), 1)
    tril = r_i >= c_i
    acum = jnp.dot(tril.astype(F32), da, precision=lax.Precision.HIGHEST,
                   preferred_element_type=F32)
    acum_t = acum.T

    lane_t = lax.broadcasted_iota(jnp.int32, (T, 128), 1)
    lane_1 = lax.broadcasted_iota(jnp.int32, (1, 128), 1)

    def expand(v, lane):
        parts = [jnp.where(lane < SSD_HEADDIM, v[:, 2 * k:2 * k + 1], v[:, 2 * k + 1:2 * k + 2])
                 for k in range(E // 2)]
        return parts[0] if len(parts) == 1 else jnp.concatenate(parts, axis=1)

    xdt = xs * expand(dtv, lane_t)
    cb = lax.dot_general(cm.astype(BF16), bm.astype(BF16), _NT, preferred_element_type=F32)
    xdt_b = xdt.astype(BF16)
    yd = []
    for k in range(E // 2):
        pair = xdt_b[:, 2 * k * SSD_HEADDIM:(2 * k + 2) * SSD_HEADDIM]
        r = []
        for e in (2 * k, 2 * k + 1):
            seg = acum[:, e:e + 1] - acum_t[e:e + 1, :]
            g = (cb * jnp.exp(jnp.where(tril, seg, NEG))).astype(BF16)
            r.append(jnp.dot(g, pair, preferred_element_type=F32))
        yd.append(jnp.where(lane_t < SSD_HEADDIM, r[0], r[1]))
    y_diag = yd[0] if len(yd) == 1 else jnp.concatenate(yd, axis=1)

    a_last = acum[T - 1:T, :]
    h_t = state_ref[...]
    y_off = jnp.dot(cm.astype(BF16), h_t.astype(BF16), preferred_element_type=F32)
    y_off = y_off * expand(jnp.exp(acum), lane_t)
    xw = (xdt * expand(jnp.exp(a_last - acum), lane_t)).astype(BF16)
    state_ref[...] = h_t * expand(jnp.exp(a_last), lane_1) + jnp.dot(
        bm.T.astype(BF16), xw, preferred_element_type=F32)

    y = y_diag + y_off + dskip_ref[...] * xs
    y = y * _silu(z_ref[...])
    ms = jnp.mean(y * y, axis=-1, keepdims=True)
    o_ref[...] = (y * lax.rsqrt(ms + NORM_EPS) * ng_ref[...]).astype(o_ref.dtype)


def ssd_mixer(proj, dtg, conv_w, conv_b, dt_bias, a_log, d_skip, norm_g, d_model):
    lp = proj.shape[0]
    d = d_model
    heads = d // SSD_HEADDIM
    e = heads // SSD_GROUPS
    w = d // SSD_GROUPS
    n = SSD_STATE
    t = SSD_ROWS
    xb, bb, cb = d // w, 2 * d // n, (2 * d + SSD_GROUPS * n) // n

    def padg(v):
        return jnp.pad(v.astype(F32).reshape(SSD_GROUPS, e), ((0, 0), (0, 128 - e))).reshape(1, -1)

    cw = conv_w.astype(F32)
    cbias = conv_b.astype(F32).reshape(1, -1)
    dsk = jnp.repeat(d_skip.astype(F32), SSD_HEADDIM).reshape(1, d)
    return pl.pallas_call(
        functools.partial(_ssd_kernel, T=t, E=e),
        grid=(SSD_GROUPS, lp // t),
        in_specs=[
            pl.BlockSpec((t, w), lambda g, c: (c, g)),
            pl.BlockSpec((t, w), lambda g, c: (c, xb + g)),
            pl.BlockSpec((t, n), lambda g, c: (c, bb + g)),
            pl.BlockSpec((t, n), lambda g, c: (c, cb + g)),
            pl.BlockSpec((t, 128), lambda g, c: (c, g)),
            pl.BlockSpec((SSD_CONV, w), lambda g, c: (0, g)),
            pl.BlockSpec((SSD_CONV, n), lambda g, c: (0, d // n + g)),
            pl.BlockSpec((SSD_CONV, n), lambda g, c: (0, (d + SSD_GROUPS * n) // n + g)),
            pl.BlockSpec((1, w), lambda g, c: (0, g)),
            pl.BlockSpec((1, n), lambda g, c: (0, d // n + g)),
            pl.BlockSpec((1, n), lambda g, c: (0, (d + SSD_GROUPS * n) // n + g)),
            pl.BlockSpec((1, 128), lambda g, c: (0, g)),
            pl.BlockSpec((1, 128), lambda g, c: (0, g)),
            pl.BlockSpec((1, w), lambda g, c: (0, g)),
            pl.BlockSpec((1, w), lambda g, c: (0, g)),
        ],
        out_specs=pl.BlockSpec((t, w), lambda g, c: (c, g)),
        out_shape=jax.ShapeDtypeStruct((lp, d), BF16),
        scratch_shapes=[pltpu.VMEM((t + 8, w + 2 * n), F32), pltpu.VMEM((n, w), F32)],
        compiler_params=_cparams("parallel", "arbitrary"),
        name="ssd_mixer",
    )(proj, proj, proj, proj, dtg, cw, cw, cw, cbias, cbias, cbias, padg(dt_bias), padg(a_log),
      dsk, norm_g.astype(F32).reshape(1, d))


def _s5_kernel(u_ref, trev_ref, bs_ref, cs_ref, ar_ref, ai_ref, y_ref, in_ref, sp_ref, st_ref, *, NBT):
    kb = S5_BLOCK
    half = S5_GB * S5_STATE

    @pl.when(pl.program_id(1) == 0)
    def _():
        st_ref[...] = jnp.zeros_like(st_ref)

    xcat = jnp.concatenate([u_ref[pl.ds(tt, NBT, stride=kb), :].astype(BF16) for tt in range(kb)], axis=1)
    in_ref[...] = jnp.dot(xcat, bs_ref[0], preferred_element_type=F32)

    ar = ar_ref[0]
    ai = ai_ref[0]

    def body(b, carry):
        re, im = carry
        row = pl.ds(b, 1)
        sp_ref[row, 0:half] = re
        sp_ref[row, half:2 * half] = im
        return (ar * re - ai * im + in_ref[row, 0:half], ar * im + ai * re + in_ref[row, half:2 * half])

    re, im = lax.fori_loop(0, NBT, body, (st_ref[0], st_ref[1]))
    st_ref[0] = re
    st_ref[1] = im

    y_state = jnp.dot(sp_ref[...].astype(BF16), cs_ref[0], preferred_element_type=F32)
    lanes = S5_GB * S5_GROUP
    for tt in range(kb):
        y = jnp.dot(xcat[:, 0:(tt + 1) * lanes], trev_ref[0, (kb - 1 - tt) * lanes:kb * lanes, :],
                    preferred_element_type=F32)
        y_ref[pl.ds(tt, NBT, stride=kb), :] = y + y_state[:, tt * lanes:(tt + 1) * lanes]


def _s5_operators(a_re, a_im, log_dt, b_re, b_im, c_re, c_im):
    hi = lax.Precision.HIGHEST
    g, p = a_re.shape
    kb = S5_BLOCK
    a_re, a_im = a_re.astype(F32), a_im.astype(F32)
    b_re, b_im = b_re.astype(F32), b_im.astype(F32)
    c_re, c_im = c_re.astype(F32), c_im.astype(F32)
    dt = jnp.exp(log_dt.astype(F32))[:, None]
    mag = jnp.exp(dt * a_re)
    ab_re, ab_im = mag * jnp.cos(dt * a_im), mag * jnp.sin(dt * a_im)
    den = a_re * a_re + a_im * a_im
    k_re = ((ab_re - 1.0) * a_re + ab_im * a_im) / den
    k_im = (ab_im * a_re - (ab_re - 1.0) * a_im) / den
    bb_re = k_re[..., None] * b_re - k_im[..., None] * b_im
    bb_im = k_re[..., None] * b_im + k_im[..., None] * b_re
    ks = jnp.arange(kb + 1, dtype=F32)[:, None, None]
    pm = jnp.exp(ks * (dt * a_re))
    pr, pi = pm * jnp.cos(ks * (dt * a_im)), pm * jnp.sin(ks * (dt * a_im))
    abr = pr[..., None] * bb_re - pi[..., None] * bb_im
    abi = pr[..., None] * bb_im + pi[..., None] * bb_re
    mk = (jnp.einsum('gjp,kgpi->kgji', c_re, abr[:kb], precision=hi)
          - jnp.einsum('gjp,kgpi->kgji', c_im, abi[:kb], precision=hi))
    car = c_re[None] * pr[1:, :, None, :] - c_im[None] * pi[1:, :, None, :]
    cai = c_re[None] * pi[1:, :, None, :] + c_im[None] * pr[1:, :, None, :]
    gb = S5_GB
    nblk = g // gb
    lanes = gb * S5_GROUP

    def blk(x):
        return x.reshape(x.shape[0], nblk, gb, *x.shape[2:])

    def eye_at(axis_l, axis_m, ndim):
        shape = [1] * ndim
        shape[axis_l] = shape[axis_m] = gb
        return jnp.eye(gb, dtype=BF16).reshape(shape)

    t_c = blk(mk[::-1]).transpose(1, 0, 2, 4, 3).astype(BF16)
    trev = (t_c[:, :, :, :, None, :] * eye_at(2, 4, 6)).reshape(nblk, kb * lanes, lanes)
    b_c = jnp.stack([blk(x[:kb][::-1]) for x in (abr, abi)], axis=0)
    b_c = b_c.transpose(2, 1, 3, 5, 0, 4).astype(BF16)
    bs = (b_c[:, :, :, :, :, None, :] * eye_at(2, 5, 7)).reshape(nblk, kb * lanes, 2 * gb * p)
    c_c = jnp.stack([blk(x) for x in (car, -cai)], axis=0)
    c_c = c_c.transpose(2, 0, 3, 5, 1, 4).astype(BF16)
    cs = (c_c[:, :, :, :, :, None, :] * eye_at(2, 5, 7)).reshape(nblk, 2 * gb * p, kb * lanes)
    ar = pr[kb].reshape(nblk, 1, gb * p)
    ai = pi[kb].reshape(nblk, 1, gb * p)
    return trev, bs, cs, ar, ai


def s5_ssm(proj, ops, u_off, s5w):
    trev, bs, cs, ar, ai = ops
    lp = proj.shape[0]
    lanes = S5_GB * S5_GROUP
    states = 2 * S5_GB * S5_STATE
    nblk = s5w // lanes
    nb = lp // S5_BLOCK
    nbt = _pick(nb, (208, 40))
    rows = nbt * S5_BLOCK
    return pl.pallas_call(
        functools.partial(_s5_kernel, NBT=nbt),
        grid=(nblk, nb // nbt),
        in_specs=[
            pl.BlockSpec((rows, lanes), lambda i, j: (j, u_off // lanes + i)),
            pl.BlockSpec((1, S5_BLOCK * lanes, lanes), lambda i, j: (i, 0, 0)),
            pl.BlockSpec((1, S5_BLOCK * lanes, states), lambda i, j: (i, 0, 0)),
            pl.BlockSpec((1, states, S5_BLOCK * lanes), lambda i, j: (i, 0, 0)),
            pl.BlockSpec((1, 1, states // 2), lambda i, j: (i, 0, 0)),
            pl.BlockSpec((1, 1, states // 2), lambda i, j: (i, 0, 0)),
        ],
        out_specs=pl.BlockSpec((rows, lanes), lambda i, j: (j, i)),
        out_shape=jax.ShapeDtypeStruct((lp, s5w), F32),
        scratch_shapes=[pltpu.VMEM((nbt, states), F32), pltpu.VMEM((nbt, states), F32),
                        pltpu.VMEM((2, 1, states // 2), F32)],
        compiler_params=_cparams("parallel", "arbitrary"),
        name="s5_ssm",
    )(proj, trev, bs, cs, ar, ai)


def _glu_kernel(y_ref, u_ref, gate_ref, d_ref, w_ref, o_ref):
    y = y_ref[...] + d_ref[...] * u_ref[...]
    c0 = math.sqrt(2.0 / math.pi)
    g = 0.5 * y * (1.0 + jnp.tanh(c0 * (y + 0.044715 * (y * y * y))))
    lin = jnp.dot(g.astype(BF16), w_ref[...], preferred_element_type=F32)
    o_ref[...] = (g * _sigmoid(lin) * _silu(gate_ref[...])).astype(o_ref.dtype)


def s5_glu(y_ssm, proj, d_skip, w_glu, u_off, gate_off):
    lp, sw = y_ssm.shape
    tm = _pick(lp, (MM_ROWS,))
    return pl.pallas_call(
        _glu_kernel,
        grid=(lp // tm,),
        in_specs=[
            pl.BlockSpec((tm, sw), lambda i: (i, 0)),
            pl.BlockSpec((tm, sw), lambda i: (i, u_off // sw)),
            pl.BlockSpec((tm, sw), lambda i: (i, gate_off // sw)),
            pl.BlockSpec((1, sw), lambda i: (0, 0)),
            pl.BlockSpec((sw, sw), lambda i: (0, 0)),
        ],
        out_specs=pl.BlockSpec((tm, sw), lambda i: (i, 0)),
        out_shape=jax.ShapeDtypeStruct((lp, sw), BF16),
        compiler_params=_cparams("parallel"),
        name="s5_glu",
    )(y_ssm, proj, proj, d_skip.astype(F32).reshape(1, sw), w_glu.astype(BF16))


def _rope(r, cos, sin):
    half = MLA_ROPE // 2
    x1, x2 = r[:, :half], r[:, half:]
    return jnp.concatenate([x1 * cos - x2 * sin, x2 * cos + x1 * sin], axis=1)


def _q_kernel(cq_ref, g_ref, w_ref, cos_ref, sin_ref, o_ref, xn_ref, *, scale, HPS):
    @pl.when(pl.program_id(1) == 0)
    def _():
        x = cq_ref[...]
        ms = jnp.mean(x * x, axis=-1, keepdims=True)
        xn_ref[...] = (x * lax.rsqrt(ms + NORM_EPS) * g_ref[...]).astype(BF16)

    acc = lax.dot_general(w_ref[0], xn_ref[...], _NT, preferred_element_type=F32)
    tm = acc.shape[1]
    half = MLA_ROPE // 2
    cos, sin = cos_ref[...], sin_ref[...]
    tail_row = lax.broadcasted_iota(jnp.int32, (MLA_QK_PAD - MLA_NOPE - MLA_ROPE, tm), 0)
    tail = jnp.where(tail_row == 0, 1.0, 0.0)
    for hh in range(HPS):
        qn = acc[hh * MLA_NOPE:(hh + 1) * MLA_NOPE, :]
        r0 = HPS * MLA_NOPE + hh * MLA_ROPE
        x1, x2 = acc[r0:r0 + half, :], acc[r0 + half:r0 + MLA_ROPE, :]
        qr = jnp.concatenate([x1 * cos - x2 * sin, x2 * cos + x1 * sin], axis=0)
        o_ref[hh] = jnp.concatenate([qn * scale, qr * scale, tail], axis=0).astype(o_ref.dtype)


def _kv_kernel(ckv_ref, g_ref, wk_ref, wvt_ref, kr_ref, cos_ref, sin_ref, k_ref, vt_ref, xn_ref,
               *, HPS, TK):
    @pl.when(pl.program_id(1) == 0)
    def _():
        x = ckv_ref[...]
        ms = jnp.mean(x * x, axis=-1, keepdims=True)
        xn_ref[...] = (x * lax.rsqrt(ms + NORM_EPS) * g_ref[...]).astype(BF16)

    xn = xn_ref[...]
    k = jnp.dot(xn, wk_ref[0], preferred_element_type=F32)
    tm = k.shape[0]
    kr = _rope(kr_ref[...][:, 0:MLA_ROPE], cos_ref[...], sin_ref[...])
    tail_shape = (tm, MLA_QK_PAD - MLA_NOPE - MLA_ROPE)
    rows = pl.program_id(0) * tm + lax.broadcasted_iota(jnp.int32, tail_shape, 0)
    lane = lax.broadcasted_iota(jnp.int32, tail_shape, 1)
    pad = jnp.where((lane == 0) & (rows < FRONT_PAD), NEG, 0.0)
    vt = lax.dot_general(wvt_ref[0], xn, _NT, preferred_element_type=F32)
    ones_row = lax.broadcasted_iota(jnp.int32, (MLA_VT_ROWS - MLA_V, tm), 0)
    ones = jnp.where(ones_row == 0, 1.0, 0.0)
    for hh in range(HPS):
        kh = jnp.concatenate([k[:, hh * MLA_NOPE:(hh + 1) * MLA_NOPE], kr, pad], axis=1).astype(k_ref.dtype)
        vh = jnp.concatenate([vt[hh * MLA_V:(hh + 1) * MLA_V, :], ones], axis=0).astype(vt_ref.dtype)
        for t in range(tm // TK):
            k_ref[hh, t] = kh[t * TK:(t + 1) * TK, :]
            vt_ref[hh, t] = vh[:, t * TK:(t + 1) * TK]


def _attn_kernel(q_ref, k_ref, vt_ref, gate_ref, o_ref, s_ref, mx_ref, m_ref, acc_ref, *, TK, NQ):
    iq = pl.program_id(1)
    k_row = lax.broadcasted_iota(jnp.int32, (TK, TK), 0)
    q_col = lax.broadcasted_iota(jnp.int32, (TK, TK), 1)
    diag_ok = (k_row // CHUNK) <= (q_col // CHUNK)

    def with_max(s, masked):
        if masked:
            s = jnp.where(diag_ok, s, NEG)
        return s, jnp.max(s, axis=0, keepdims=True)

    def scores(kt, c, masked=False):
        return with_max(jnp.dot(kt, q_ref[0, :, c * TK:(c + 1) * TK], preferred_element_type=F32), masked)

    def update(c, s_mx, vt):
        s, mx = s_mx
        m = m_ref[c]
        m_new = jnp.maximum(m, mx)
        alpha = jnp.exp2(m - m_new)
        p = jnp.exp2(s - m_new)
        m_ref[c] = m_new
        acc_ref[c] = alpha * acc_ref[c] + jnp.dot(vt, p.astype(BF16), preferred_element_type=F32)

    def stash(s_all):
        for c in range(NQ):
            s_ref[c], mx_ref[c] = s_all[c]

    def fetch():
        return [(s_ref[c], mx_ref[c]) for c in range(NQ)]

    m_ref[...] = jnp.full(m_ref.shape, NEG, F32)
    acc_ref[...] = jnp.zeros(acc_ref.shape, F32)
    n_full = NQ * iq
    stash([scores(k_ref[0, 0], c) for c in range(NQ)])

    def full_tile(j, s_now):
        kt_next = k_ref[0, j + 1]
        vt = vt_ref[0, j]
        s_next = [scores(kt_next, 0)]
        for c in range(NQ):
            if c + 1 < NQ:
                s_next.append(scores(kt_next, c + 1))
            update(c, s_now[c], vt)
        return s_next

    def multi_body(jj, _):
        s_now = fetch()
        for t in range(ATT_UNROLL):
            s_now = full_tile(ATT_UNROLL * jj + t, s_now)
        stash(s_now)
        return 0

    def single_body(j, _):
        stash(full_tile(j, fetch()))
        return 0

    n_multi = n_full // ATT_UNROLL
    lax.fori_loop(0, n_multi, multi_body, 0)
    lax.fori_loop(n_multi * ATT_UNROLL, n_full, single_body, 0)

    s_d = fetch()
    s_d[0] = with_max(s_d[0][0], True)
    for d in range(NQ):
        if d + 1 < NQ:
            kt = k_ref[0, n_full + d + 1]
            s_next = [None] * (d + 1) + [scores(kt, c, masked=(c == d + 1)) for c in range(d + 1, NQ)]
        vt = vt_ref[0, n_full + d]
        for c in range(d, NQ):
            update(c, s_d[c], vt)
        s_d = s_next
    for c in range(NQ):
        a = acc_ref[c]
        o = (a[0:MLA_V, :] * (1.0 / a[MLA_V:MLA_V + 1, :])).T
        rows = slice(c * TK, (c + 1) * TK)
        o_ref[rows, :] = (o * _silu(gate_ref[rows, :])).astype(o_ref.dtype)


def mla_mixer(proj, kr_raw, q_norm, w_uq, kv_norm, w_ukv, d_model, row_pos):
    lp = proj.shape[0]
    d = d_model
    h = d // MLA_V
    qr, kvr = d // 4, d // 8
    ta = ATT_ROWS
    nt = lp // ta
    half = MLA_ROPE // 2
    inv_freq = ROPE_BASE ** (-jnp.arange(0, MLA_ROPE, 2, dtype=F32) / MLA_ROPE)
    ang = row_pos[:, None] * inv_freq[None, :]
    cos, sin = jnp.cos(ang), jnp.sin(ang)

    hps = MLA_HPS
    ng = h // hps
    wq = w_uq.reshape(qr, ng, hps, MLA_NOPE + MLA_ROPE)
    wq = jnp.concatenate([wq[..., :MLA_NOPE].reshape(qr, ng, hps * MLA_NOPE),
                          wq[..., MLA_NOPE:].reshape(qr, ng, hps * MLA_ROPE)], axis=-1)
    wqt = wq.transpose(1, 2, 0).astype(BF16)
    wkv = w_ukv.reshape(kvr, ng, hps, MLA_NOPE + MLA_V)
    wk = wkv[..., :MLA_NOPE].reshape(kvr, ng, hps * MLA_NOPE).transpose(1, 0, 2).astype(BF16)
    wvt = wkv[..., MLA_NOPE:].reshape(kvr, ng, hps * MLA_V).transpose(1, 2, 0).astype(BF16)

    tmq = ATT_Q_TILES * ta
    scale = (MLA_NOPE + MLA_ROPE) ** -0.5 * math.log2(math.e)
    qt3 = pl.pallas_call(
        functools.partial(_q_kernel, scale=scale, HPS=hps),
        grid=(lp // tmq, ng),
        in_specs=[
            pl.BlockSpec((tmq, qr), lambda i, j: (i, 0)),
            pl.BlockSpec((1, qr), lambda i, j: (0, 0)),
            pl.BlockSpec((1, hps * (MLA_NOPE + MLA_ROPE), qr), lambda i, j: (j, 0, 0)),
            pl.BlockSpec((half, tmq), lambda i, j: (0, i)),
            pl.BlockSpec((half, tmq), lambda i, j: (0, i)),
        ],
        out_specs=pl.BlockSpec((hps, MLA_QK_PAD, tmq), lambda i, j: (j, 0, i)),
        out_shape=jax.ShapeDtypeStruct((h, MLA_QK_PAD, lp), BF16),
        scratch_shapes=[pltpu.VMEM((tmq, qr), BF16)],
        compiler_params=_cparams("parallel", "arbitrary"),
        name="mla_q",
    )(proj, q_norm.astype(F32).reshape(1, qr), wqt, cos.T, sin.T)

    tpk = tmq // ta
    k4, vt4 = pl.pallas_call(
        functools.partial(_kv_kernel, HPS=hps, TK=ta),
        grid=(lp // tmq, ng),
        in_specs=[
            pl.BlockSpec((tmq, kvr), lambda i, j: (i, qr // kvr)),
            pl.BlockSpec((1, kvr), lambda i, j: (0, 0)),
            pl.BlockSpec((1, kvr, hps * MLA_NOPE), lambda i, j: (j, 0, 0)),
            pl.BlockSpec((1, hps * MLA_V, kvr), lambda i, j: (j, 0, 0)),
            pl.BlockSpec((tmq, 128), lambda i, j: (i, 0)),
            pl.BlockSpec((tmq, half), lambda i, j: (i, 0)),
            pl.BlockSpec((tmq, half), lambda i, j: (i, 0)),
        ],
        out_specs=[pl.BlockSpec((hps, tpk, ta, MLA_QK_PAD), lambda i, j: (j, i, 0, 0)),
                   pl.BlockSpec((hps, tpk, MLA_VT_ROWS, ta), lambda i, j: (j, i, 0, 0))],
        out_shape=[jax.ShapeDtypeStruct((h, nt, ta, MLA_QK_PAD), BF16),
                   jax.ShapeDtypeStruct((h, nt, MLA_VT_ROWS, ta), BF16)],
        scratch_shapes=[pltpu.VMEM((tmq, kvr), BF16)],
        compiler_params=_cparams("parallel", "arbitrary"),
        name="mla_kv",
    )(proj, kv_norm.astype(F32).reshape(1, kvr), wk, wvt, kr_raw, cos, sin)

    gate_blk = (qr + kvr) // MLA_V
    nq = ATT_Q_TILES
    tq = nq * ta
    return pl.pallas_call(
        functools.partial(_attn_kernel, TK=ta, NQ=nq),
        grid=(h, lp // tq),
        in_specs=[
            pl.BlockSpec((1, MLA_QK_PAD, tq), lambda hh, i: (hh, 0, i)),
            pl.BlockSpec((1, nt, ta, MLA_QK_PAD), lambda hh, i: (hh, 0, 0, 0)),
            pl.BlockSpec((1, nt, MLA_VT_ROWS, ta), lambda hh, i: (hh, 0, 0, 0)),
            pl.BlockSpec((tq, MLA_V), lambda hh, i: (i, gate_blk + hh)),
        ],
        out_specs=pl.BlockSpec((tq, MLA_V), lambda hh, i: (i, hh)),
        out_shape=jax.ShapeDtypeStruct((lp, d), BF16),
        scratch_shapes=[pltpu.VMEM((nq, ta, ta), F32), pltpu.VMEM((nq, 1, ta), F32),
                        pltpu.VMEM((nq, 1, ta), F32), pltpu.VMEM((nq, MLA_VT_ROWS, ta), F32)],
        compiler_params=_cparams("parallel", "parallel"),
        name="mla_attention",
    )(qt3, k4, vt4, proj)


def _hybrid_layer(h, norm_g, w_in, conv_w, conv_b, dt_bias, a_log, ssd_d, ssd_norm, s5_ops, s5_d,
                  w_glu, w_out):
    lp, d = h.shape
    heads = d // SSD_HEADDIM
    e = heads // SSD_GROUPS
    conv_dim = d + 2 * SSD_GROUPS * SSD_STATE
    s5w = d // 2
    o_dt = d + conv_dim
    w_main = jnp.concatenate([w_in[:, :o_dt], w_in[:, o_dt + heads:]], axis=1).astype(BF16)
    w_dt = w_in[:, o_dt:o_dt + heads].reshape(d, SSD_GROUPS, e)
    w_dt = jnp.pad(w_dt, ((0, 0), (0, 0), (0, 128 // SSD_GROUPS - e))).reshape(d, 128).astype(BF16)
    proj, dt_raw = norm_proj(h, norm_g, w_main, w_dt, "hyb_in_proj")
    dtg = jnp.pad(dt_raw.reshape(lp, SSD_GROUPS, 128 // SSD_GROUPS),
                  ((0, 0), (0, 0), (0, 128 - 128 // SSD_GROUPS))).reshape(lp, SSD_GROUPS * 128)
    y_a = ssd_mixer(proj, dtg, conv_w, conv_b, dt_bias, a_log, ssd_d, ssd_norm, d)

    u_off = o_dt
    gate_off = o_dt + s5w
    y_ssm = s5_ssm(proj, s5_ops, u_off, s5w)
    y_b = s5_glu(y_ssm, proj, s5_d, w_glu, u_off, gate_off)
    w_o = w_out.astype(BF16)
    return matmul([y_a, y_b], [w_o[:d], w_o[d:]], F32, res=h, name="hyb_out_proj")


def _mla_layer(h, norm_g, w_in, q_norm, w_uq, kv_norm, w_ukv, w_out, row_pos):
    lp, d = h.shape
    qr, kvr = d // 4, d // 8
    o_kr = qr + kvr
    w_main = jnp.concatenate([w_in[:, :o_kr], w_in[:, o_kr + MLA_ROPE:]], axis=1).astype(BF16)
    w_kr = jnp.pad(w_in[:, o_kr:o_kr + MLA_ROPE], ((0, 0), (0, 128 - MLA_ROPE))).astype(BF16)
    proj, kr_raw = norm_proj(h, norm_g, w_main, w_kr, "mla_in_proj")
    y_c = mla_mixer(proj, kr_raw, q_norm, w_uq, kv_norm, w_ukv, d, row_pos)
    return matmul([y_c], [w_out.astype(BF16)], F32, res=h, name="mla_out_proj")


def kernel(x, meta, hyb_norm, hyb_w_in, ssd_conv_w, ssd_conv_b, ssd_dt_bias, ssd_a_log, ssd_d, ssd_norm, s5_a_re, s5_a_im, s5_log_dt, s5_b_re, s5_b_im, s5_c_re, s5_c_im, s5_d, s5_w_glu, hyb_w_out, mla_norm, mla_w_in, mla_q_norm, mla_w_uq, mla_kv_norm, mla_w_ukv, mla_w_out, final_norm):
    bsz, seq, d = x.shape
    assert bsz == 1 and seq % CHUNK == 0 and meta.shape == (N_META, d)
    depth = hyb_norm.shape[0] + mla_norm.shape[0]
    first = FRONT_PAD + N_META
    lp = -(-(first + seq) // ROW_ALIGN) * ROW_ALIGN
    h = jnp.concatenate([jnp.zeros((FRONT_PAD, d), F32), meta.astype(F32), x[0].astype(F32),
                         jnp.zeros((lp - first - seq, d), F32)], axis=0)
    row_pos = jnp.arange(lp, dtype=jnp.int32).astype(F32) - float(FRONT_PAD)
    for layer in range(depth):
        i = layer // 2
        if layer % 2 == 0:
            ops = _s5_operators(s5_a_re[i], s5_a_im[i], s5_log_dt[i], s5_b_re[i], s5_b_im[i],
                                s5_c_re[i], s5_c_im[i])
            h = _hybrid_layer(h, hyb_norm[i], hyb_w_in[i], ssd_conv_w[i], ssd_conv_b[i],
                              ssd_dt_bias[i], ssd_a_log[i], ssd_d[i], ssd_norm[i], ops, s5_d[i],
                              s5_w_glu[i], hyb_w_out[i])
        else:
            h = _mla_layer(h, mla_norm[i], mla_w_in[i], mla_q_norm[i], mla_w_uq[i], mla_kv_norm[i],
                           mla_w_ukv[i], mla_w_out[i], row_pos)
    out = rms_norm_rows(h, final_norm, x.dtype, CHUNK, row_block_offset=first // CHUNK, n_rows=seq)
    return out.reshape(1, seq, d)
```

```python
import functools
import math

import jax
import jax.numpy as jnp
from jax import lax
from jax.experimental import pallas as pl
from jax.experimental.pallas import tpu as pltpu

F32 = jnp.float32
BF16 = jnp.bfloat16

CHUNK = 64
N_META = 16
FRONT_PAD = (-N_META) % CHUNK
NORM_EPS = 1e-6

SSD_HEADDIM = 64
SSD_GROUPS = 8
SSD_STATE = 128
SSD_CONV = 4

S5_GROUP = 16
S5_STATE = 64
S5_BLOCK = 16
S5_GB = 8

MLA_NOPE = 128
MLA_ROPE = 64
MLA_V = 128
MLA_QK_PAD = 256
MLA_VT_ROWS = MLA_V + 16
ROPE_BASE = 10000.0

ROW_ALIGN = 1280
MM_ROWS = 640
NORM_ROWS = 256
SSD_ROWS = 128
ATT_ROWS = 256
ATT_Q_TILES = 5
ATT_UNROLL = 8
MLA_HPS = 4
NEG = -1e30
VMEM_LIMIT = 56 * 1024 * 1024

_NT = (((1,), (1,)), ((), ()))


def _cparams(*sem):
    return pltpu.CompilerParams(dimension_semantics=sem, vmem_limit_bytes=VMEM_LIMIT)


def _silu(x):
    return x * (1.0 / (1.0 + jnp.exp(-x)))


def _sigmoid(x):
    return 1.0 / (1.0 + jnp.exp(-x))


def _softplus(x):
    return jnp.maximum(x, 0.0) + jnp.log(1.0 + jnp.exp(-jnp.abs(x)))


def _pick(n, cands):
    for c in cands:
        if n % c == 0:
            return c
    raise ValueError(f"no tile for {n} in {cands}")


def _rms_kernel(x_ref, g_ref, o_ref):
    x = x_ref[...]
    ms = jnp.mean(x * x, axis=-1, keepdims=True)
    o_ref[...] = (x * lax.rsqrt(ms + NORM_EPS) * g_ref[...]).astype(o_ref.dtype)


def rms_norm_rows(x, g, out_dtype, rows, row_block_offset=0, n_rows=None):
    n, d = x.shape
    n_rows = n if n_rows is None else n_rows
    return pl.pallas_call(
        _rms_kernel,
        grid=(n_rows // rows,),
        in_specs=[pl.BlockSpec((rows, d), lambda i: (i + row_block_offset, 0)),
                  pl.BlockSpec((1, d), lambda i: (0, 0))],
        out_specs=pl.BlockSpec((rows, d), lambda i: (i, 0)),
        out_shape=jax.ShapeDtypeStruct((n_rows, d), out_dtype),
        compiler_params=_cparams("parallel"),
        name="rms_norm",
    )(x, g.reshape(1, d).astype(F32))


def _mm_kernel(*refs, nx, has_res):
    o_ref = refs[-1]
    acc = None
    for xr, wr in zip(refs[:nx], refs[nx:2 * nx]):
        d = jnp.dot(xr[...], wr[...], preferred_element_type=F32)
        acc = d if acc is None else acc + d
    if has_res:
        acc = acc + refs[2 * nx][...]
    o_ref[...] = acc.astype(o_ref.dtype)


def matmul(xs, ws, out_dtype, res=None, name="matmul"):
    m = xs[0].shape[0]
    n = ws[0].shape[1]
    tm = _pick(m, (MM_ROWS,))
    tn = _pick(n, (512, 256, 128))
    in_specs = [pl.BlockSpec((tm, x.shape[1]), lambda i, j: (i, 0)) for x in xs]
    in_specs += [pl.BlockSpec((w.shape[0], tn), lambda i, j: (0, j)) for w in ws]
    args = list(xs) + list(ws)
    if res is not None:
        in_specs.append(pl.BlockSpec((tm, tn), lambda i, j: (i, j)))
        args.append(res)
    return pl.pallas_call(
        functools.partial(_mm_kernel, nx=len(xs), has_res=res is not None),
        grid=(m // tm, n // tn),
        in_specs=in_specs,
        out_specs=pl.BlockSpec((tm, tn), lambda i, j: (i, j)),
        out_shape=jax.ShapeDtypeStruct((m, n), out_dtype),
        compiler_params=_cparams("parallel", "parallel"),
        name=name,
    )(*args)


def _norm_proj_kernel(x_ref, g_ref, w_ref, ws_ref, o_ref, os_ref, xn_ref):
    @pl.when(pl.program_id(1) == 0)
    def _():
        x = x_ref[...]
        ms = jnp.mean(x * x, axis=-1, keepdims=True)
        xn_ref[...] = (x * lax.rsqrt(ms + NORM_EPS) * g_ref[...]).astype(BF16)
        os_ref[...] = jnp.dot(xn_ref[...], ws_ref[...], preferred_element_type=F32)

    o_ref[...] = jnp.dot(xn_ref[...], w_ref[...], preferred_element_type=F32)


def norm_proj(x, g, w_main, w_side, name):
    m, d = x.shape
    n = w_main.shape[1]
    tm = _pick(m, (MM_ROWS,))
    tn = _pick(n, (512, 256, 128))
    ns = w_side.shape[1]
    return pl.pallas_call(
        _norm_proj_kernel,
        grid=(m // tm, n // tn),
        in_specs=[pl.BlockSpec((tm, d), lambda i, j: (i, 0)),
                  pl.BlockSpec((1, d), lambda i, j: (0, 0)),
                  pl.BlockSpec((d, tn), lambda i, j: (0, j)),
                  pl.BlockSpec((d, ns), lambda i, j: (0, 0))],
        out_specs=[pl.BlockSpec((tm, tn), lambda i, j: (i, j)),
                   pl.BlockSpec((tm, ns), lambda i, j: (i, 0))],
        out_shape=[jax.ShapeDtypeStruct((m, n), F32), jax.ShapeDtypeStruct((m, ns), F32)],
        scratch_shapes=[pltpu.VMEM((tm, d), BF16)],
        compiler_params=_cparams("parallel", "arbitrary"),
        name=name,
    )(x, g.reshape(1, d).astype(F32), w_main, w_side)


def _ssd_kernel(z_ref, x_ref, b_ref, c_ref, dt_ref, cwx_ref, cwb_ref, cwc_ref, cbx_ref, cbb_ref,
                cbc_ref, dtb_ref, alog_ref, dskip_ref, ng_ref, o_ref, ext_ref, state_ref, *, T, E):
    c = pl.program_id(1)
    W = E * SSD_HEADDIM
    N = SSD_STATE

    @pl.when(c == 0)
    def _():
        ext_ref[0:8, :] = jnp.zeros((8, W + 2 * N), F32)
        state_ref[...] = jnp.zeros_like(state_ref)

    @pl.when(c > 0)
    def _():
        ext_ref[0:8, :] = ext_ref[T:T + 8, :]

    ext_ref[8:T + 8, 0:W] = x_ref[...]
    ext_ref[8:T + 8, W:W + N] = b_ref[...]
    ext_ref[8:T + 8, W + N:W + 2 * N] = c_ref[...]

    cw = jnp.concatenate([cwx_ref[...], cwb_ref[...], cwc_ref[...]], axis=1)
    acc = jnp.concatenate([cbx_ref[...], cbb_ref[...], cbc_ref[...]], axis=1)
    for k in range(SSD_CONV):
        acc = acc + cw[k:k + 1, :] * ext_ref[pl.ds(8 - (SSD_CONV - 1) + k, T), :]
    xc = _silu(acc)
    xs = xc[:, 0:W]
    bm = xc[:, W:W + N]
    cm = xc[:, W + N:W + 2 * N]

    lanes_per_group = 128 // SSD_GROUPS
    shift = (128 - lanes_per_group * pl.program_id(0)) % 128
    rows = c * T + lax.broadcasted_iota(jnp.int32, (T, 128), 0)
    dtv = _softplus(pltpu.roll(dt_ref[...], shift, axis=1) + dtb_ref[...])
    dtv = jnp.where(rows >= FRONT_PAD, dtv, 0.0)
    a = -jnp.exp(alog_ref[...])
    da = dtv * a
    r_i = lax.broadcasted_iota(jnp.int32, (T, T), 0)
    c_i = lax.broadcasted_iota(jnp.int32, (T, T), 1)
    tril = r_i >= c_i
    acum = jnp.dot(tril.astype(F32), da, precision=lax.Precision.HIGHEST,
                   preferred_element_type=F32)
    acum_t = acum.T

    lane_t = lax.broadcasted_iota(jnp.int32, (T, 128), 1)
    lane_1 = lax.broadcasted_iota(jnp.int32, (1, 128), 1)

    def expand(v, lane):
        parts = [jnp.where(lane < SSD_HEADDIM, v[:, 2 * k:2 * k + 1], v[:, 2 * k + 1:2 * k + 2])
                 for k in range(E // 2)]
        return parts[0] if len(parts) == 1 else jnp.concatenate(parts, axis=1)

    xdt = xs * expand(dtv, lane_t)
    cb = lax.dot_general(cm.astype(BF16), bm.astype(BF16), _NT, preferred_element_type=F32)
    xdt_b = xdt.astype(BF16)
    yd = []
    for k in range(E // 2):
        pair = xdt_b[:, 2 * k * SSD_HEADDIM:(2 * k + 2) * SSD_HEADDIM]
        r = []
        for e in (2 * k, 2 * k + 1):
            seg = acum[:, e:e + 1] - acum_t[e:e + 1, :]
            g = (cb * jnp.exp(jnp.where(tril, seg, NEG))).astype(BF16)
            r.append(jnp.dot(g, pair, preferred_element_type=F32))
        yd.append(jnp.where(lane_t < SSD_HEADDIM, r[0], r[1]))
    y_diag = yd[0] if len(yd) == 1 else jnp.concatenate(yd, axis=1)

    a_last = acum[T - 1:T, :]
    h_t = state_ref[...]
    y_off = jnp.dot(cm.astype(BF16), h_t.astype(BF16), preferred_element_type=F32)
    y_off = y_off * expand(jnp.exp(acum), lane_t)
    xw = (xdt * expand(jnp.exp(a_last - acum), lane_t)).astype(BF16)
    state_ref[...] = h_t * expand(jnp.exp(a_last), lane_1) + jnp.dot(
        bm.T.astype(BF16), xw, preferred_element_type=F32)

    y = y_diag + y_off + dskip_ref[...] * xs
    y = y * _silu(z_ref[...])
    ms = jnp.mean(y * y, axis=-1, keepdims=True)
    o_ref[...] = (y * lax.rsqrt(ms + NORM_EPS) * ng_ref[...]).astype(o_ref.dtype)


def ssd_mixer(proj, dt_raw, conv_w, conv_b, dt_bias, a_log, d_skip, norm_g, d_model):
    lp = proj.shape[0]
    d = d_model
    heads = d // SSD_HEADDIM
    e = heads // SSD_GROUPS
    w = d // SSD_GROUPS
    n = SSD_STATE
    t = SSD_ROWS
    xb, bb, cb = d // w, 2 * d // n, (2 * d + SSD_GROUPS * n) // n

    def padg(v):
        return jnp.pad(v.astype(F32).reshape(SSD_GROUPS, e), ((0, 0), (0, 128 - e))).reshape(1, -1)

    cw = conv_w.astype(F32)
    cbias = conv_b.astype(F32).reshape(1, -1)
    dsk = jnp.repeat(d_skip.astype(F32), SSD_HEADDIM).reshape(1, d)
    return pl.pallas_call(
        functools.partial(_ssd_kernel, T=t, E=e),
        grid=(SSD_GROUPS, lp // t),
        in_specs=[
            pl.BlockSpec((t, w), lambda g, c: (c, g)),
            pl.BlockSpec((t, w), lambda g, c: (c, xb + g)),
            pl.BlockSpec((t, n), lambda g, c: (c, bb + g)),
            pl.BlockSpec((t, n), lambda g, c: (c, cb + g)),
            pl.BlockSpec((t, 128), lambda g, c: (c, 0)),
            pl.BlockSpec((SSD_CONV, w), lambda g, c: (0, g)),
            pl.BlockSpec((SSD_CONV, n), lambda g, c: (0, d // n + g)),
            pl.BlockSpec((SSD_CONV, n), lambda g, c: (0, (d + SSD_GROUPS * n) // n + g)),
            pl.BlockSpec((1, w), lambda g, c: (0, g)),
            pl.BlockSpec((1, n), lambda g, c: (0, d // n + g)),
            pl.BlockSpec((1, n), lambda g, c: (0, (d + SSD_GROUPS * n) // n + g)),
            pl.BlockSpec((1, 128), lambda g, c: (0, g)),
            pl.BlockSpec((1, 128), lambda g, c: (0, g)),
            pl.BlockSpec((1, w), lambda g, c: (0, g)),
            pl.BlockSpec((1, w), lambda g, c: (0, g)),
        ],
        out_specs=pl.BlockSpec((t, w), lambda g, c: (c, g)),
        out_shape=jax.ShapeDtypeStruct((lp, d), BF16),
        scratch_shapes=[pltpu.VMEM((t + 8, w + 2 * n), F32), pltpu.VMEM((n, w), F32)],
        compiler_params=_cparams("parallel", "arbitrary"),
        name="ssd_mixer",
    )(proj, proj, proj, proj, dt_raw, cw, cw, cw, cbias, cbias, cbias, padg(dt_bias), padg(a_log),
      dsk, norm_g.astype(F32).reshape(1, d))


def _s5_kernel(u_ref, trev_ref, bs_ref, cs_ref, ar_ref, ai_ref, y_ref, in_ref, sp_ref, st_ref, *, NBT):
    kb = S5_BLOCK
    half = S5_GB * S5_STATE

    @pl.when(pl.program_id(1) == 0)
    def _():
        st_ref[...] = jnp.zeros_like(st_ref)

    xcat = jnp.concatenate([u_ref[pl.ds(tt, NBT, stride=kb), :].astype(BF16) for tt in range(kb)], axis=1)
    in_ref[...] = lax.dot_general(xcat, bs_ref[0], _NT, preferred_element_type=F32)

    ar = ar_ref[0]
    ai = ai_ref[0]

    def body(b, carry):
        re, im = carry
        row = pl.ds(b, 1)
        sp_ref[row, 0:half] = re
        sp_ref[row, half:2 * half] = im
        return (ar * re - ai * im + in_ref[row, 0:half], ar * im + ai * re + in_ref[row, half:2 * half])

    re, im = lax.fori_loop(0, NBT, body, (st_ref[0], st_ref[1]))
    st_ref[0] = re
    st_ref[1] = im

    y_state = lax.dot_general(sp_ref[...].astype(BF16), cs_ref[0], _NT, preferred_element_type=F32)
    lanes = S5_GB * S5_GROUP
    for tt in range(kb):
        y = jnp.dot(xcat[:, 0:(tt + 1) * lanes], trev_ref[0, (kb - 1 - tt) * lanes:kb * lanes, :],
                    preferred_element_type=F32)
        y_ref[pl.ds(tt, NBT, stride=kb), :] = y + y_state[:, tt * lanes:(tt + 1) * lanes]


def _s5_operators(a_re, a_im, log_dt, b_re, b_im, c_re, c_im):
    hi = lax.Precision.HIGHEST
    g, p = a_re.shape
    kb = S5_BLOCK
    a_re, a_im = a_re.astype(F32), a_im.astype(F32)
    b_re, b_im = b_re.astype(F32), b_im.astype(F32)
    c_re, c_im = c_re.astype(F32), c_im.astype(F32)
    dt = jnp.exp(log_dt.astype(F32))[:, None]
    mag = jnp.exp(dt * a_re)
    ab_re, ab_im = mag * jnp.cos(dt * a_im), mag * jnp.sin(dt * a_im)
    den = a_re * a_re + a_im * a_im
    k_re = ((ab_re - 1.0) * a_re + ab_im * a_im) / den
    k_im = (ab_im * a_re - (ab_re - 1.0) * a_im) / den
    bb_re = k_re[..., None] * b_re - k_im[..., None] * b_im
    bb_im = k_re[..., None] * b_im + k_im[..., None] * b_re
    ks = jnp.arange(kb + 1, dtype=F32)[:, None, None]
    pm = jnp.exp(ks * (dt * a_re))
    pr, pi = pm * jnp.cos(ks * (dt * a_im)), pm * jnp.sin(ks * (dt * a_im))
    abr = pr[..., None] * bb_re - pi[..., None] * bb_im
    abi = pr[..., None] * bb_im + pi[..., None] * bb_re
    mk = (jnp.einsum('gjp,kgpi->kgji', c_re, abr[:kb], precision=hi)
          - jnp.einsum('gjp,kgpi->kgji', c_im, abi[:kb], precision=hi))
    car = c_re[None] * pr[1:, :, None, :] - c_im[None] * pi[1:, :, None, :]
    cai = c_re[None] * pi[1:, :, None, :] + c_im[None] * pr[1:, :, None, :]
    gb = S5_GB
    nblk = g // gb
    lanes = gb * S5_GROUP

    def blk(x):
        return x.reshape(x.shape[0], nblk, gb, *x.shape[2:])

    def group_of(n, width):
        return (jnp.arange(n, dtype=jnp.int32) // width) % gb

    l_idx = jnp.arange(gb, dtype=jnp.int32)
    t_c = blk(mk[::-1]).transpose(1, 0, 2, 4, 3).astype(BF16)
    t_w = jnp.broadcast_to(t_c[:, :, :, :, None, :], (nblk, kb, gb, S5_GROUP, gb, S5_GROUP))
    t_w = t_w.reshape(nblk, kb, gb, S5_GROUP, lanes)
    trev = jnp.where(l_idx[None, None, :, None, None] == group_of(lanes, S5_GROUP), t_w, 0)
    trev = trev.reshape(nblk, kb * lanes, lanes)
    b_c = jnp.stack([blk(x[:kb][::-1]) for x in (abr, abi)], axis=0)
    b_c = b_c.transpose(2, 0, 4, 1, 3, 5).astype(BF16).reshape(nblk, 2, p, kb * lanes)
    bs = jnp.where(l_idx[None, None, :, None, None] == group_of(kb * lanes, S5_GROUP),
                   b_c[:, :, None, :, :], 0).reshape(nblk, 2 * gb * p, kb * lanes)
    c_c = jnp.stack([blk(x) for x in (car, -cai)], axis=0)
    c_c = c_c.transpose(2, 1, 4, 0, 3, 5).astype(BF16).reshape(nblk, kb, S5_GROUP, 2 * gb * p)
    cs = jnp.where(l_idx[None, None, :, None, None] == group_of(2 * gb * p, p),
                   c_c[:, :, None, :, :], 0).reshape(nblk, kb * lanes, 2 * gb * p)
    ar = pr[kb].reshape(nblk, 1, gb * p)
    ai = pi[kb].reshape(nblk, 1, gb * p)
    return trev, bs, cs, ar, ai


def s5_ssm(proj, ops, u_off, s5w):
    trev, bs, cs, ar, ai = ops
    lp = proj.shape[0]
    lanes = S5_GB * S5_GROUP
    states = 2 * S5_GB * S5_STATE
    nblk = s5w // lanes
    nb = lp // S5_BLOCK
    nbt = _pick(nb, (208, 40))
    rows = nbt * S5_BLOCK
    return pl.pallas_call(
        functools.partial(_s5_kernel, NBT=nbt),
        grid=(nblk, nb // nbt),
        in_specs=[
            pl.BlockSpec((rows, lanes), lambda i, j: (j, u_off // lanes + i)),
            pl.BlockSpec((1, S5_BLOCK * lanes, lanes), lambda i, j: (i, 0, 0)),
            pl.BlockSpec((1, states, S5_BLOCK * lanes), lambda i, j: (i, 0, 0)),
            pl.BlockSpec((1, S5_BLOCK * lanes, states), lambda i, j: (i, 0, 0)),
            pl.BlockSpec((1, 1, states // 2), lambda i, j: (i, 0, 0)),
            pl.BlockSpec((1, 1, states // 2), lambda i, j: (i, 0, 0)),
        ],
        out_specs=pl.BlockSpec((rows, lanes), lambda i, j: (j, i)),
        out_shape=jax.ShapeDtypeStruct((lp, s5w), F32),
        scratch_shapes=[pltpu.VMEM((nbt, states), F32), pltpu.VMEM((nbt, states), F32),
                        pltpu.VMEM((2, 1, states // 2), F32)],
        compiler_params=_cparams("parallel", "arbitrary"),
        name="s5_ssm",
    )(proj, trev, bs, cs, ar, ai)


def _glu_kernel(y_ref, u_ref, gate_ref, d_ref, w_ref, o_ref):
    y = y_ref[...] + d_ref[...] * u_ref[...]
    c0 = math.sqrt(2.0 / math.pi)
    g = 0.5 * y * (1.0 + jnp.tanh(c0 * (y + 0.044715 * (y * y * y))))
    lin = jnp.dot(g.astype(BF16), w_ref[...], preferred_element_type=F32)
    o_ref[...] = (g * _sigmoid(lin) * _silu(gate_ref[...])).astype(o_ref.dtype)


def s5_glu(y_ssm, proj, d_skip, w_glu, u_off, gate_off):
    lp, sw = y_ssm.shape
    tm = _pick(lp, (MM_ROWS,))
    return pl.pallas_call(
        _glu_kernel,
        grid=(lp // tm,),
        in_specs=[
            pl.BlockSpec((tm, sw), lambda i: (i, 0)),
            pl.BlockSpec((tm, sw), lambda i: (i, u_off // sw)),
            pl.BlockSpec((tm, sw), lambda i: (i, gate_off // sw)),
            pl.BlockSpec((1, sw), lambda i: (0, 0)),
            pl.BlockSpec((sw, sw), lambda i: (0, 0)),
        ],
        out_specs=pl.BlockSpec((tm, sw), lambda i: (i, 0)),
        out_shape=jax.ShapeDtypeStruct((lp, sw), BF16),
        compiler_params=_cparams("parallel"),
        name="s5_glu",
    )(y_ssm, proj, proj, d_skip.astype(F32).reshape(1, sw), w_glu.astype(BF16))


def _rope(r, cos, sin):
    half = MLA_ROPE // 2
    x1, x2 = r[:, :half], r[:, half:]
    return jnp.concatenate([x1 * cos - x2 * sin, x2 * cos + x1 * sin], axis=1)


def _q_kernel(cq_ref, g_ref, w_ref, cos_ref, sin_ref, o_ref, xn_ref, *, scale, HPS):
    @pl.when(pl.program_id(1) == 0)
    def _():
        x = cq_ref[...]
        ms = jnp.mean(x * x, axis=-1, keepdims=True)
        xn_ref[...] = (x * lax.rsqrt(ms + NORM_EPS) * g_ref[...]).astype(BF16)

    acc = lax.dot_general(w_ref[0], xn_ref[...], _NT, preferred_element_type=F32)
    tm = acc.shape[1]
    half = MLA_ROPE // 2
    cos, sin = cos_ref[...], sin_ref[...]
    tail_row = lax.broadcasted_iota(jnp.int32, (MLA_QK_PAD - MLA_NOPE - MLA_ROPE, tm), 0)
    tail = jnp.where(tail_row == 0, 1.0, 0.0)
    for hh in range(HPS):
        qn = acc[hh * MLA_NOPE:(hh + 1) * MLA_NOPE, :]
        r0 = HPS * MLA_NOPE + hh * MLA_ROPE
        x1, x2 = acc[r0:r0 + half, :], acc[r0 + half:r0 + MLA_ROPE, :]
        qr = jnp.concatenate([x1 * cos - x2 * sin, x2 * cos + x1 * sin], axis=0)
        o_ref[hh] = jnp.concatenate([qn * scale, qr * scale, tail], axis=0).astype(o_ref.dtype)


def _kv_kernel(ckv_ref, g_ref, wk_ref, wvt_ref, kr_ref, cos_ref, sin_ref, k_ref, vt_ref, xn_ref,
               *, HPS, TK):
    @pl.when(pl.program_id(1) == 0)
    def _():
        x = ckv_ref[...]
        ms = jnp.mean(x * x, axis=-1, keepdims=True)
        xn_ref[...] = (x * lax.rsqrt(ms + NORM_EPS) * g_ref[...]).astype(BF16)

    xn = xn_ref[...]
    k = jnp.dot(xn, wk_ref[0], preferred_element_type=F32)
    tm = k.shape[0]
    kr = _rope(kr_ref[...][:, 0:MLA_ROPE], cos_ref[...], sin_ref[...])
    tail_shape = (tm, MLA_QK_PAD - MLA_NOPE - MLA_ROPE)
    rows = pl.program_id(0) * tm + lax.broadcasted_iota(jnp.int32, tail_shape, 0)
    lane = lax.broadcasted_iota(jnp.int32, tail_shape, 1)
    pad = jnp.where((lane == 0) & (rows < FRONT_PAD), NEG, 0.0)
    vt = lax.dot_general(wvt_ref[0], xn, _NT, preferred_element_type=F32)
    ones_row = lax.broadcasted_iota(jnp.int32, (MLA_VT_ROWS - MLA_V, tm), 0)
    ones = jnp.where(ones_row == 0, 1.0, 0.0)
    for hh in range(HPS):
        kh = jnp.concatenate([k[:, hh * MLA_NOPE:(hh + 1) * MLA_NOPE], kr, pad], axis=1).astype(k_ref.dtype)
        vh = jnp.concatenate([vt[hh * MLA_V:(hh + 1) * MLA_V, :], ones], axis=0).astype(vt_ref.dtype)
        for t in range(tm // TK):
            k_ref[hh, t] = kh[t * TK:(t + 1) * TK, :]
            vt_ref[hh, t] = vh[:, t * TK:(t + 1) * TK]


def _attn_kernel(q_ref, k_ref, vt_ref, gate_ref, o_ref, s_ref, mx_ref, m_ref, acc_ref, *, TK, NQ):
    iq = pl.program_id(1)
    k_row = lax.broadcasted_iota(jnp.int32, (TK, TK), 0)
    q_col = lax.broadcasted_iota(jnp.int32, (TK, TK), 1)
    diag_ok = (k_row // CHUNK) <= (q_col // CHUNK)

    def with_max(s, masked):
        if masked:
            s = jnp.where(diag_ok, s, NEG)
        return s, jnp.max(s, axis=0, keepdims=True)

    def scores(kt, c, masked=False):
        return with_max(jnp.dot(kt, q_ref[0, :, c * TK:(c + 1) * TK], preferred_element_type=F32), masked)

    def update(c, s_mx, vt):
        s, mx = s_mx
        m = m_ref[c]
        m_new = jnp.maximum(m, mx)
        alpha = jnp.exp2(m - m_new)
        p = jnp.exp2(s - m_new)
        m_ref[c] = m_new
        acc_ref[c] = alpha * acc_ref[c] + jnp.dot(vt, p.astype(BF16), preferred_element_type=F32)

    def stash(s_all):
        for c in range(NQ):
            s_ref[c], mx_ref[c] = s_all[c]

    def fetch():
        return [(s_ref[c], mx_ref[c]) for c in range(NQ)]

    m_ref[...] = jnp.full(m_ref.shape, NEG, F32)
    acc_ref[...] = jnp.zeros(acc_ref.shape, F32)
    n_full = NQ * iq
    stash([scores(k_ref[0, 0], c) for c in range(NQ)])

    def full_tile(j, s_now):
        kt_next = k_ref[0, j + 1]
        vt = vt_ref[0, j]
        s_next = [scores(kt_next, 0)]
        for c in range(NQ):
            if c + 1 < NQ:
                s_next.append(scores(kt_next, c + 1))
            update(c, s_now[c], vt)
        return s_next

    def multi_body(jj, _):
        s_now = fetch()
        for t in range(ATT_UNROLL):
            s_now = full_tile(ATT_UNROLL * jj + t, s_now)
        stash(s_now)
        return 0

    def single_body(j, _):
        stash(full_tile(j, fetch()))
        return 0

    n_multi = n_full // ATT_UNROLL
    lax.fori_loop(0, n_multi, multi_body, 0)
    lax.fori_loop(n_multi * ATT_UNROLL, n_full, single_body, 0)

    s_d = fetch()
    s_d[0] = with_max(s_d[0][0], True)
    for d in range(NQ):
        if d + 1 < NQ:
            kt = k_ref[0, n_full + d + 1]
            s_next = [None] * (d + 1) + [scores(kt, c, masked=(c == d + 1)) for c in range(d + 1, NQ)]
        vt = vt_ref[0, n_full + d]
        for c in range(d, NQ):
            update(c, s_d[c], vt)
        s_d = s_next
    for c in range(NQ):
        a = acc_ref[c]
        o = (a[0:MLA_V, :] * (1.0 / a[MLA_V:MLA_V + 1, :])).T
        rows = slice(c * TK, (c + 1) * TK)
        o_ref[rows, :] = (o * _silu(gate_ref[rows, :])).astype(o_ref.dtype)


def mla_mixer(proj, kr_raw, q_norm, w_uq, kv_norm, w_ukv, d_model, row_pos):
    lp = proj.shape[0]
    d = d_model
    h = d // MLA_V
    qr, kvr = d // 4, d // 8
    ta = ATT_ROWS
    nt = lp // ta
    half = MLA_ROPE // 2
    inv_freq = ROPE_BASE ** (-jnp.arange(0, MLA_ROPE, 2, dtype=F32) / MLA_ROPE)
    ang = row_pos[:, None] * inv_freq[None, :]
    cos, sin = jnp.cos(ang), jnp.sin(ang)

    hps = MLA_HPS
    ng = h // hps
    wq = w_uq.reshape(qr, ng, hps, MLA_NOPE + MLA_ROPE)
    wq = jnp.concatenate([wq[..., :MLA_NOPE].reshape(qr, ng, hps * MLA_NOPE),
                          wq[..., MLA_NOPE:].reshape(qr, ng, hps * MLA_ROPE)], axis=-1)
    wqt = wq.transpose(1, 2, 0).astype(BF16)
    wkv = w_ukv.reshape(kvr, ng, hps, MLA_NOPE + MLA_V)
    wk = wkv[..., :MLA_NOPE].reshape(kvr, ng, hps * MLA_NOPE).transpose(1, 0, 2).astype(BF16)
    wvt = wkv[..., MLA_NOPE:].reshape(kvr, ng, hps * MLA_V).transpose(1, 2, 0).astype(BF16)

    tmq = ATT_Q_TILES * ta
    scale = (MLA_NOPE + MLA_ROPE) ** -0.5 * math.log2(math.e)
    qt3 = pl.pallas_call(
        functools.partial(_q_kernel, scale=scale, HPS=hps),
        grid=(lp // tmq, ng),
        in_specs=[
            pl.BlockSpec((tmq, qr), lambda i, j: (i, 0)),
            pl.BlockSpec((1, qr), lambda i, j: (0, 0)),
            pl.BlockSpec((1, hps * (MLA_NOPE + MLA_ROPE), qr), lambda i, j: (j, 0, 0)),
            pl.BlockSpec((half, tmq), lambda i, j: (0, i)),
            pl.BlockSpec((half, tmq), lambda i, j: (0, i)),
        ],
        out_specs=pl.BlockSpec((hps, MLA_QK_PAD, tmq), lambda i, j: (j, 0, i)),
        out_shape=jax.ShapeDtypeStruct((h, MLA_QK_PAD, lp), BF16),
        scratch_shapes=[pltpu.VMEM((tmq, qr), BF16)],
        compiler_params=_cparams("parallel", "arbitrary"),
        name="mla_q",
    )(proj, q_norm.astype(F32).reshape(1, qr), wqt, cos.T, sin.T)

    tpk = tmq // ta
    k4, vt4 = pl.pallas_call(
        functools.partial(_kv_kernel, HPS=hps, TK=ta),
        grid=(lp // tmq, ng),
        in_specs=[
            pl.BlockSpec((tmq, kvr), lambda i, j: (i, qr // kvr)),
            pl.BlockSpec((1, kvr), lambda i, j: (0, 0)),
            pl.BlockSpec((1, kvr, hps * MLA_NOPE), lambda i, j: (j, 0, 0)),
            pl.BlockSpec((1, hps * MLA_V, kvr), lambda i, j: (j, 0, 0)),
            pl.BlockSpec((tmq, 128), lambda i, j: (i, 0)),
            pl.BlockSpec((tmq, half), lambda i, j: (i, 0)),
            pl.BlockSpec((tmq, half), lambda i, j: (i, 0)),
        ],
        out_specs=[pl.BlockSpec((hps, tpk, ta, MLA_QK_PAD), lambda i, j: (j, i, 0, 0)),
                   pl.BlockSpec((hps, tpk, MLA_VT_ROWS, ta), lambda i, j: (j, i, 0, 0))],
        out_shape=[jax.ShapeDtypeStruct((h, nt, ta, MLA_QK_PAD), BF16),
                   jax.ShapeDtypeStruct((h, nt, MLA_VT_ROWS, ta), BF16)],
        scratch_shapes=[pltpu.VMEM((tmq, kvr), BF16)],
        compiler_params=_cparams("parallel", "arbitrary"),
        name="mla_kv",
    )(proj, kv_norm.astype(F32).reshape(1, kvr), wk, wvt, kr_raw, cos, sin)

    gate_blk = (qr + kvr) // MLA_V
    nq = ATT_Q_TILES
    tq = nq * ta
    return pl.pallas_call(
        functools.partial(_attn_kernel, TK=ta, NQ=nq),
        grid=(h, lp // tq),
        in_specs=[
            pl.BlockSpec((1, MLA_QK_PAD, tq), lambda hh, i: (hh, 0, i)),
            pl.BlockSpec((1, nt, ta, MLA_QK_PAD), lambda hh, i: (hh, 0, 0, 0)),
            pl.BlockSpec((1, nt, MLA_VT_ROWS, ta), lambda hh, i: (hh, 0, 0, 0)),
            pl.BlockSpec((tq, MLA_V), lambda hh, i: (i, gate_blk + hh)),
        ],
        out_specs=pl.BlockSpec((tq, MLA_V), lambda hh, i: (i, hh)),
        out_shape=jax.ShapeDtypeStruct((lp, d), BF16),
        scratch_shapes=[pltpu.VMEM((nq, ta, ta), F32), pltpu.VMEM((nq, 1, ta), F32),
                        pltpu.VMEM((nq, 1, ta), F32), pltpu.VMEM((nq, MLA_VT_ROWS, ta), F32)],
        compiler_params=_cparams("parallel", "parallel"),
        name="mla_attention",
    )(qt3, k4, vt4, proj)


def _hybrid_layer(h, norm_g, w_in, conv_w, conv_b, dt_bias, a_log, ssd_d, ssd_norm, s5_ops, s5_d,
                  w_glu, w_out):
    lp, d = h.shape
    heads = d // SSD_HEADDIM
    e = heads // SSD_GROUPS
    conv_dim = d + 2 * SSD_GROUPS * SSD_STATE
    s5w = d // 2
    o_dt = d + conv_dim
    w_main = jnp.concatenate([w_in[:, :o_dt], w_in[:, o_dt + heads:]], axis=1).astype(BF16)
    w_dt = w_in[:, o_dt:o_dt + heads].reshape(d, SSD_GROUPS, e)
    w_dt = jnp.pad(w_dt, ((0, 0), (0, 0), (0, 128 // SSD_GROUPS - e))).reshape(d, 128).astype(BF16)
    proj, dt_raw = norm_proj(h, norm_g, w_main, w_dt, "hyb_in_proj")
    y_a = ssd_mixer(proj, dt_raw, conv_w, conv_b, dt_bias, a_log, ssd_d, ssd_norm, d)

    u_off = o_dt
    gate_off = o_dt + s5w
    y_ssm = s5_ssm(proj, s5_ops, u_off, s5w)
    y_b = s5_glu(y_ssm, proj, s5_d, w_glu, u_off, gate_off)
    w_o = w_out.astype(BF16)
    return matmul([y_a, y_b], [w_o[:d], w_o[d:]], F32, res=h, name="hyb_out_proj")


def _mla_layer(h, norm_g, w_in, q_norm, w_uq, kv_norm, w_ukv, w_out, row_pos):
    lp, d = h.shape
    qr, kvr = d // 4, d // 8
    o_kr = qr + kvr
    w_main = jnp.concatenate([w_in[:, :o_kr], w_in[:, o_kr + MLA_ROPE:]], axis=1).astype(BF16)
    w_kr = jnp.pad(w_in[:, o_kr:o_kr + MLA_ROPE], ((0, 0), (0, 128 - MLA_ROPE))).astype(BF16)
    proj, kr_raw = norm_proj(h, norm_g, w_main, w_kr, "mla_in_proj")
    y_c = mla_mixer(proj, kr_raw, q_norm, w_uq, kv_norm, w_ukv, d, row_pos)
    return matmul([y_c], [w_out.astype(BF16)], F32, res=h, name="mla_out_proj")


def kernel(x, meta, hyb_norm, hyb_w_in, ssd_conv_w, ssd_conv_b, ssd_dt_bias, ssd_a_log, ssd_d, ssd_norm, s5_a_re, s5_a_im, s5_log_dt, s5_b_re, s5_b_im, s5_c_re, s5_c_im, s5_d, s5_w_glu, hyb_w_out, mla_norm, mla_w_in, mla_q_norm, mla_w_uq, mla_kv_norm, mla_w_ukv, mla_w_out, final_norm):
    bsz, seq, d = x.shape
    assert bsz == 1 and seq % CHUNK == 0 and meta.shape == (N_META, d)
    depth = hyb_norm.shape[0] + mla_norm.shape[0]
    first = FRONT_PAD + N_META
    lp = -(-(first + seq) // ROW_ALIGN) * ROW_ALIGN
    h = jnp.concatenate([jnp.zeros((FRONT_PAD, d), F32), meta.astype(F32), x[0].astype(F32),
                         jnp.zeros((lp - first - seq, d), F32)], axis=0)
    row_pos = jnp.arange(lp, dtype=jnp.int32).astype(F32) - float(FRONT_PAD)
    for layer in range(depth):
        i = layer // 2
        if layer % 2 == 0:
            ops = _s5_operators(s5_a_re[i], s5_a_im[i], s5_log_dt[i], s5_b_re[i], s5_b_im[i],
                                s5_c_re[i], s5_c_im[i])
            h = _hybrid_layer(h, hyb_norm[i], hyb_w_in[i], ssd_conv_w[i], ssd_conv_b[i],
                              ssd_dt_bias[i], ssd_a_log[i], ssd_d[i], ssd_norm[i], ops, s5_d[i],
                              s5_w_glu[i], hyb_w_out[i])
        else:
            h = _mla_layer(h, mla_norm[i], mla_w_in[i], mla_q_norm[i], mla_w_uq[i], mla_kv_norm[i],
                           mla_w_ukv[i], mla_w_out[i], row_pos)
    out = rms_norm_rows(h, final_norm, x.dtype, CHUNK, row_block_offset=first // CHUNK, n_rows=seq)
    return out.reshape(1, seq, d)
```
